```python
import math
import jax, jax.numpy as jnp
from jax import lax
import numpy as np

D_MODEL = 1024
BATCH = 8
SEQ = 4096
DEPTH = 4

MEM_LEN = 256
A_HEADS = 8
A_HEAD_DIM = 64
IDX_HEADS = 4
IDX_DIM = 64
TOPK_MAX = 256
B_HEADS = 4
B_QK_DIM = 64
B_V_DIM = 2 * B_QK_DIM
MIX_WIDTH = A_HEADS * A_HEAD_DIM + B_HEADS * B_V_DIM
MEM_HEADS = 4
MEM_HEAD_DIM = D_MODEL // MEM_HEADS
D_FF = 2816
CONV_WIDTH = 3
ROPE_THETA = 500000.0
ROPE_FRACTION = 4
Q_BLOCK = 128
EPS = 1e-6
SPLITS = (A_HEADS * A_HEAD_DIM, A_HEAD_DIM, A_HEAD_DIM,
          IDX_HEADS * IDX_DIM, IDX_DIM, IDX_HEADS,
          2 * B_HEADS * B_QK_DIM, 2 * B_HEADS * B_QK_DIM, B_HEADS * B_V_DIM)
IN_WIDTH = sum(SPLITS)

kernel_name = 'hybrid_dsa_diffattn_convglu_trunk'


def _rms_norm(x, g):
    xf = x.astype(jnp.float32)
    y = xf * lax.rsqrt(jnp.mean(xf * xf, axis=-1, keepdims=True) + EPS)
    return (y * g.astype(jnp.float32)).astype(x.dtype)


def _rope_tables(positions, rot_dim):
    inv_freq = ROPE_THETA ** (-jnp.arange(0, rot_dim, 2, dtype=jnp.float32) / rot_dim)
    ang = positions.astype(jnp.float32)[..., None] * inv_freq
    return jnp.cos(ang), jnp.sin(ang)


def _partial_rope(t, cos, sin):
    rot = t.shape[-1] // ROPE_FRACTION
    half = rot // 2
    tr = t[..., :rot].astype(jnp.float32)
    t1, t2 = tr[..., :half], tr[..., half:]
    c = cos[:, :, None, :]
    s = sin[:, :, None, :]
    rotated = jnp.concatenate([t1 * c - t2 * s, t2 * c + t1 * s], axis=-1).astype(t.dtype)
    return jnp.concatenate([rotated, t[..., rot:]], axis=-1)


def _to_blocks(t):
    b, s = t.shape[:2]
    return t.reshape((b, s // Q_BLOCK, Q_BLOCK) + t.shape[2:]).swapaxes(0, 1)


def _from_blocks(t):
    nb, b, qb = t.shape[:3]
    return t.swapaxes(0, 1).reshape((b, nb * qb) + t.shape[3:])


def _dsa_attention(q_a, k_a, v_a, q_idx, k_idx, w_idx):
    seq = q_a.shape[1]
    n_sel = min(TOPK_MAX, seq // 4)
    key_pos = jnp.arange(seq)
    q_pos = key_pos.reshape(seq // Q_BLOCK, Q_BLOCK)

    def block(args):
        qa_b, qi_b, wi_b, qp = args
        causal = key_pos[None, :] <= qp[:, None]
        dots = jnp.einsum('bqhd,bkd->bqhk', qi_b, k_idx,
                          preferred_element_type=jnp.float32) * IDX_DIM ** -0.5
        score = jnp.einsum('bqhk,bqh->bqk', jax.nn.relu(dots),
                           wi_b.astype(jnp.float32)) * IDX_HEADS ** -0.5
        score = jnp.where(causal[None], score, -jnp.inf)
        _, sel = lax.top_k(score, n_sel)
        k_sel = jax.vmap(lambda t, i: t[i])(k_a, sel)
        v_sel = jax.vmap(lambda t, i: t[i])(v_a, sel)
        valid = sel <= qp[None, :, None]
        logits = jnp.einsum('bqhd,bqkd->bqhk', qa_b, k_sel,
                            preferred_element_type=jnp.float32) * A_HEAD_DIM ** -0.5
        logits = jnp.where(valid[:, :, None, :], logits, -jnp.inf)
        p = jax.nn.softmax(logits, axis=-1)
        return jnp.einsum('bqhk,bqkd->bqhd', p.astype(v_sel.dtype), v_sel)

    out = lax.map(block, (_to_blocks(q_a), _to_blocks(q_idx), _to_blocks(w_idx), q_pos))
    return _from_blocks(out)


def _diff_attention(q_b, k_b, v_b, lam):
    seq = q_b.shape[1]
    key_pos = jnp.arange(seq)
    q_pos = key_pos.reshape(seq // Q_BLOCK, Q_BLOCK)

    def block(args):
        qb_blk, qp = args
        causal = key_pos[None, :] <= qp[:, None]
        logits = jnp.einsum('bqhcd,bkhcd->bhcqk', qb_blk, k_b,
                            preferred_element_type=jnp.float32) * B_QK_DIM ** -0.5
        logits = jnp.where(causal, logits, -jnp.inf)
        p = jax.nn.softmax(logits, axis=-1)
        a = p[:, :, 0] - lam * p[:, :, 1]
        return jnp.einsum('bhqk,bkhe->bqhe', a.astype(v_b.dtype), v_b)

    return _from_blocks(lax.map(block, (_to_blocks(q_b), q_pos)))


def _memory_attention(xn, memn, w_q, w_kv, w_o):
    b, s, _ = xn.shape
    m = memn.shape[1]
    q = (xn @ w_q).reshape(b, s, MEM_HEADS, MEM_HEAD_DIM)
    k, v = jnp.split(memn @ w_kv, 2, axis=-1)
    k = k.reshape(b, m, MEM_HEADS, MEM_HEAD_DIM)
    v = v.reshape(b, m, MEM_HEADS, MEM_HEAD_DIM)
    logits = jnp.einsum('bqhd,bkhd->bhqk', q, k,
                        preferred_element_type=jnp.float32) * MEM_HEAD_DIM ** -0.5
    p = jax.nn.softmax(logits, axis=-1)
    o = jnp.einsum('bhqk,bkhd->bqhd', p.astype(v.dtype), v).reshape(b, s, D_MODEL)
    return o @ w_o


def _conv_glu(xn, w_gate_up, conv_w, conv_b, w_down):
    g, u = jnp.split(xn @ w_gate_up, 2, axis=-1)
    g = lax.conv_general_dilated(
        g, conv_w[:, None, :].astype(g.dtype), window_strides=(1,),
        padding=[(CONV_WIDTH - 1, 0)], dimension_numbers=('NWC', 'WIO', 'NWC'),
        feature_group_count=D_FF) + conv_b
    return (jax.nn.silu(g) * u) @ w_down


def setup_inputs(seed: int = 0) -> dict:
    key = jax.random.key(seed)
    ks = jax.random.split(key, 24)
    f32 = jnp.float32

    def nrm(k, shape, scale):
        return jax.random.normal(k, shape, f32) * scale

    def gain(k, shape):
        return 1.0 + 0.02 * jax.random.normal(k, shape, f32)

    offset = jax.random.randint(ks[2], (BATCH, 1), 0, 1024, dtype=jnp.int32)
    positions = offset + jnp.arange(SEQ, dtype=jnp.int32)[None, :]
    return {
        'x': nrm(ks[0], (BATCH, SEQ, D_MODEL), 1.0),
        'mem': nrm(ks[1], (BATCH, MEM_LEN, D_MODEL), 1.0),
        'positions': positions,
        'mix_norm_g': gain(ks[3], (DEPTH, D_MODEL)),
        'w_in': nrm(ks[4], (DEPTH, D_MODEL, IN_WIDTH), D_MODEL ** -0.5),
        'lambda_q1': nrm(ks[5], (DEPTH, B_QK_DIM), 0.1),
        'lambda_k1': nrm(ks[6], (DEPTH, B_QK_DIM), 0.1),
        'lambda_q2': nrm(ks[7], (DEPTH, B_QK_DIM), 0.1),
        'lambda_k2': nrm(ks[8], (DEPTH, B_QK_DIM), 0.1),
        'diff_norm_g': gain(ks[9], (DEPTH, B_V_DIM)),
        'w_out': nrm(ks[10], (DEPTH, MIX_WIDTH, D_MODEL), MIX_WIDTH ** -0.5),
        'mem_attn_norm_g': gain(ks[11], (DEPTH, D_MODEL)),
        'mem_norm_g': gain(ks[12], (DEPTH, D_MODEL)),
        'w_q_mem': nrm(ks[13], (DEPTH, D_MODEL, D_MODEL), D_MODEL ** -0.5),
        'w_kv_mem': nrm(ks[14], (DEPTH, D_MODEL, 2 * D_MODEL), D_MODEL ** -0.5),
        'w_o_mem': nrm(ks[15], (DEPTH, D_MODEL, D_MODEL), D_MODEL ** -0.5),
        'ffn_norm_g': gain(ks[16], (DEPTH, D_MODEL)),
        'w_gate_up': nrm(ks[17], (DEPTH, D_MODEL, 2 * D_FF), D_MODEL ** -0.5),
        'conv_w': nrm(ks[18], (DEPTH, CONV_WIDTH, D_FF), CONV_WIDTH ** -0.5),
        'conv_b': nrm(ks[19], (DEPTH, D_FF), 0.02),
        'w_down': nrm(ks[20], (DEPTH, D_FF, D_MODEL), D_FF ** -0.5),
        'final_norm_g': gain(ks[21], (D_MODEL,)),
    }


def reference(x, mem, positions, mix_norm_g, w_in, lambda_q1, lambda_k1, lambda_q2,
              lambda_k2, diff_norm_g, w_out, mem_attn_norm_g, mem_norm_g, w_q_mem,
              w_kv_mem, w_o_mem, ffn_norm_g, w_gate_up, conv_w, conv_b, w_down,
              final_norm_g):
    b, s, _ = x.shape
    split_points = [int(v) for v in np.cumsum(SPLITS)[:-1]]
    cos, sin = _rope_tables(positions, A_HEAD_DIM // ROPE_FRACTION)
    h = x
    for l in range(DEPTH):
        xn = _rms_norm(h, mix_norm_g[l])
        proj = xn @ w_in[l]
        q_a, k_a, v_a, q_i, k_i, w_i, q_b, k_b, v_b = jnp.split(proj, split_points, axis=-1)
        q_a = _partial_rope(q_a.reshape(b, s, A_HEADS, A_HEAD_DIM), cos, sin)
        k_a = _partial_rope(k_a.reshape(b, s, 1, A_HEAD_DIM), cos, sin)[:, :, 0]
        q_i = _partial_rope(q_i.reshape(b, s, IDX_HEADS, IDX_DIM), cos, sin)
        k_i = _partial_rope(k_i.reshape(b, s, 1, IDX_DIM), cos, sin)[:, :, 0]
        q_b = _partial_rope(q_b.reshape(b, s, 2 * B_HEADS, B_QK_DIM), cos, sin)
        q_b = q_b.reshape(b, s, B_HEADS, 2, B_QK_DIM)
        k_b = _partial_rope(k_b.reshape(b, s, 2 * B_HEADS, B_QK_DIM), cos, sin)
        k_b = k_b.reshape(b, s, B_HEADS, 2, B_QK_DIM)
        v_b = v_b.reshape(b, s, B_HEADS, B_V_DIM)

        out_a = _dsa_attention(q_a, k_a, v_a, q_i, k_i, w_i)

        lam_init = 0.8 - 0.6 * math.exp(-0.3 * l)
        lam = (jnp.exp(jnp.sum(lambda_q1[l].astype(jnp.float32) * lambda_k1[l].astype(jnp.float32)))
               - jnp.exp(jnp.sum(lambda_q2[l].astype(jnp.float32) * lambda_k2[l].astype(jnp.float32)))
               + lam_init)
        out_b = _diff_attention(q_b, k_b, v_b, lam)
        out_b = _rms_norm(out_b, diff_norm_g[l]) * (1.0 - lam_init)

        mixed = jnp.concatenate([out_a.reshape(b, s, -1), out_b.reshape(b, s, -1)], axis=-1)
        h = h + mixed @ w_out[l]

        h = h + _memory_attention(_rms_norm(h, mem_attn_norm_g[l]), _rms_norm(mem, mem_norm_g[l]),
                                  w_q_mem[l], w_kv_mem[l], w_o_mem[l])

        h = h + _conv_glu(_rms_norm(h, ffn_norm_g[l]), w_gate_up[l], conv_w[l], conv_b[l], w_down[l])
    return _rms_norm(h, final_norm_g)
```

```python
import functools
import math

import jax
import jax.numpy as jnp
from jax import lax
from jax.experimental import pallas as pl
from jax.experimental.pallas import tpu as pltpu

A_HEADS = 8
HEAD_DIM = 64
IDX_HEADS = 4
TOPK_MAX = 256
B_HEADS = 4
B_V_DIM = 128
MEM_HEADS = 4
CONV_WIDTH = 3
ROPE_THETA = 500000.0
ROT_DIM = 16
EPS = 1e-6

LANES = 128
SUBLANES = 8
VMEM_LIMIT = 56 * 1024 * 1024

_CD = jnp.bfloat16
_F32 = jnp.float32
_INT_MIN = -(2 ** 31)
_NEG = -1e30

C_QA, C_QI, C_QB, C_KB, C_SA, C_SB, C_VB, C_VA, C_WI, C_END = (
    0, 512, 768, 1280, 1792, 1920, 2048, 2560, 2688, 2816)
N_ROPED = C_VB

TB_PROJ = 512
QB_DSA = 128
KC_DSA = 512
TQ_DIFF = 512
TB_MEM = 512
TB_FFN = 256


def _dot(a, b):
    return jnp.dot(a, b, preferred_element_type=_F32)


def _dot_nt(a, b):
    return lax.dot_general(a, b, (((1,), (1,)), ((), ())), preferred_element_type=_F32)


def _rms(x, g):
    return x * lax.rsqrt(jnp.mean(x * x, axis=-1, keepdims=True) + EPS) * g


def _params(*sem):
    return pltpu.CompilerParams(dimension_semantics=sem, vmem_limit_bytes=VMEM_LIMIT)


def _rope_kernel(pos_ref, freq_ref, o_ref):
    ang = pos_ref[...] * freq_ref[...]
    lane = lax.broadcasted_iota(jnp.int32, ang.shape, 1) & (HEAD_DIM - 1)
    c, s = jnp.cos(ang), jnp.sin(ang)
    half = ROT_DIM // 2
    o_ref[:, 0:LANES] = jnp.where(lane < ROT_DIM, c, 1.0)
    o_ref[:, LANES:2 * LANES] = jnp.where(lane < half, -s, 0.0)
    o_ref[:, 2 * LANES:3 * LANES] = jnp.where((lane >= half) & (lane < ROT_DIM), s, 0.0)


def _rope_tables(positions):
    n = positions.size
    tb = 1024
    half = ROT_DIM // 2
    inv_freq = ROPE_THETA ** (-jnp.arange(0, ROT_DIM, 2, dtype=_F32) / ROT_DIM)
    lane = jnp.arange(LANES) % HEAD_DIM
    freq = jnp.where(lane < ROT_DIM, inv_freq[lane % half], 0.0).astype(_F32)[None, :]
    pos = positions.astype(_F32).reshape(n, 1)
    return pl.pallas_call(
        _rope_kernel,
        grid=(n // tb,),
        in_specs=[pl.BlockSpec((tb, 1), lambda i: (i, 0)),
                  pl.BlockSpec((1, LANES), lambda i: (0, 0))],
        out_specs=pl.BlockSpec((tb, 3 * LANES), lambda i: (i, 0)),
        out_shape=jax.ShapeDtypeStruct((n, 3 * LANES), _F32),
        compiler_params=_params("parallel"),
        name="rope_tables",
    )(pos, freq)


def _in_proj_kernel(h_ref, g_ref, w_ref, rope_ref, qa_ref, qi_ref, qb_ref, kb_ref, sa_ref,
                    sb_ref, vb_ref, va_ref, wi_ref):
    xn = _rms(h_ref[...], g_ref[...]).astype(_CD)
    a = rope_ref[:, 0:LANES]
    bm = rope_ref[:, LANES:2 * LANES]
    bp = rope_ref[:, 2 * LANES:3 * LANES]

    def project(c0, c1, out_ref, rope):
        p = _dot(xn, w_ref[:, c0:c1])
        for i in range((c1 - c0) // LANES):
            xb = p[:, i * LANES:(i + 1) * LANES]
            if rope:
                xb = (xb * a + pltpu.roll(xb, LANES - ROT_DIM // 2, 1) * bm
                      + pltpu.roll(xb, ROT_DIM // 2, 1) * bp)
            out_ref[:, i * LANES:(i + 1) * LANES] = xb.astype(out_ref.dtype)

    project(C_QA, C_QI, qa_ref, True)
    project(C_QI, C_QB, qi_ref, True)
    project(C_QB, C_KB, qb_ref, True)
    project(C_KB, C_SA, kb_ref, True)
    project(C_SA, C_SB, sa_ref, True)
    project(C_SB, C_VB, sb_ref, True)
    project(C_VB, C_VA, vb_ref, False)
    pv = _dot(xn, w_ref[:, C_VA:C_WI])
    lane = lax.broadcasted_iota(jnp.int32, pv.shape, 1)
    va_ref[...] = jnp.where(lane == HEAD_DIM, 1.0, pv).astype(va_ref.dtype)
    wi_ref[...] = _dot(xn, w_ref[:, C_WI:C_END])


def _in_proj(h2d, g, w, rope):
    n, d = h2d.shape
    tb = TB_PROJ
    widths = (C_QI - C_QA, C_QB - C_QI, C_KB - C_QB, C_SA - C_KB, C_SB - C_SA, C_VB - C_SB,
              C_VA - C_VB, C_WI - C_VA)
    out_shape = [jax.ShapeDtypeStruct((n, wd), _CD) for wd in widths]
    out_shape.append(jax.ShapeDtypeStruct((n, C_END - C_WI), _F32))
    out_specs = [pl.BlockSpec((tb, s.shape[1]), lambda i: (i, 0)) for s in out_shape]
    return pl.pallas_call(
        _in_proj_kernel,
        grid=(n // tb,),
        in_specs=[pl.BlockSpec((tb, d), lambda i: (i, 0)),
                  pl.BlockSpec((1, d), lambda i: (0, 0)),
                  pl.BlockSpec((d, C_END), lambda i: (0, 0)),
                  pl.BlockSpec((tb, 3 * LANES), lambda i: (i, 0))],
        out_specs=out_specs,
        out_shape=out_shape,
        compiler_params=_params("parallel"),
        name="in_proj",
    )(h2d, g, w, rope)


def _dsa_kernel(qa_ref, qi_ref, wi_ref, sa_ref, sb_ref, va_ref, tri_ref, o_ref, keys_ref, *,
                n_sel, qb, kc):
    j = pl.program_id(1)
    q0 = j * qb
    nc = (q0 + qb + kc - 1) // kc
    low = lax.broadcasted_iota(jnp.int32, (qb, LANES), 1) < HEAD_DIM
    zero = jnp.zeros((), _CD)

    def lo(x):
        return jnp.where(low, x, zero)

    def hi(x):
        return jnp.where(low, zero, x)

    qi = qi_ref[0]
    qi_even = jnp.concatenate([lo(qi[:, 0:LANES]), lo(qi[:, LANES:2 * LANES])], axis=0)
    qi_odd = jnp.concatenate([hi(qi[:, 0:LANES]), hi(qi[:, LANES:2 * LANES])], axis=0)
    wi = wi_ref[0] * (HEAD_DIM ** -0.5 * IDX_HEADS ** -0.5)
    wb = [jnp.broadcast_to(wi[:, h:h + 1], (qb, kc)) for h in range(IDX_HEADS)]
    col_minus_row = (lax.broadcasted_iota(jnp.int32, (qb, kc), 1)
                     - lax.broadcasted_iota(jnp.int32, (qb, kc), 0))

    def score_body(c, carry):
        off = pl.multiple_of(c * kc, kc)
        sa = sa_ref[0, pl.ds(off, kc), :]
        sb = sb_ref[0, pl.ds(off, kc), :]
        de = jnp.maximum(_dot_nt(qi_even, sb), 0.0)
        do = jnp.maximum(_dot_nt(qi_odd, sa), 0.0)
        sc = de[0:qb] * wb[0] + do[0:qb] * wb[1] + de[qb:2 * qb] * wb[2] + do[qb:2 * qb] * wb[3]
        bits = lax.bitcast_convert_type(sc, jnp.int32)
        key = bits ^ ((bits >> 31) & 0x7FFFFFFF)
        key = jnp.where(col_minus_row <= q0 - off, key, _INT_MIN)
        keys_ref[:, pl.ds(off, kc)] = key
        return carry

    lax.fori_loop(0, nc, score_body, 0)

    def count_ge(cand):
        def body(c, acc):
            off = pl.multiple_of(c * kc, kc)
            ge = jnp.where(keys_ref[:, pl.ds(off, kc)] >= cand, 1.0, 0.0)
            for i in range(kc // LANES):
                acc = acc + ge[:, i * LANES:(i + 1) * LANES]
            return acc
        acc = lax.fori_loop(0, nc, body, jnp.zeros((qb, LANES), _F32))
        return jnp.sum(acc, axis=1, keepdims=True)

    def search_body(it, cur):
        cand = cur + jnp.left_shift(jnp.int32(1), 31 - it)
        return jnp.where(count_ge(cand) >= n_sel, cand, cur)

    thr = lax.fori_loop(0, 32, search_body, jnp.full((qb, 1), _INT_MIN, jnp.int32))
    need = jnp.where(thr == _INT_MIN, 0.0, n_sel - count_ge(thr + 1))

    qa = qa_ref[0]
    npair = A_HEADS // 2
    qa_even = jnp.concatenate([lo(qa[:, p * LANES:(p + 1) * LANES]) for p in range(npair)], axis=0)
    qa_odd = jnp.concatenate([hi(qa[:, p * LANES:(p + 1) * LANES]) for p in range(npair)], axis=0)
    tri = tri_ref[...]

    def attn_body(c, carry):
        ms, accs, eq_seen = carry
        off = pl.multiple_of(c * kc, kc)
        key = keys_ref[:, pl.ds(off, kc)]
        eq = key == thr
        eqf = jnp.where(eq, 1.0, 0.0)
        rank = _dot(eqf.astype(_CD), tri) + eq_seen
        sel = (key > thr) | (eq & (rank < need))
        eq_seen = eq_seen + jnp.sum(eqf, axis=1, keepdims=True)
        sa = sa_ref[0, pl.ds(off, kc), :]
        sb = sb_ref[0, pl.ds(off, kc), :]
        va = va_ref[0, pl.ds(off, kc), :]
        s_even = _dot_nt(qa_even, sa)
        s_odd = _dot_nt(qa_odd, sb)
        new_ms, new_accs = [], []
        for h in range(A_HEADS):
            src = s_even if h % 2 == 0 else s_odd
            s = jnp.where(sel, src[(h // 2) * qb:(h // 2 + 1) * qb], _NEG)
            m_new = jnp.maximum(ms[h], jnp.max(s, axis=1, keepdims=True))
            p = jnp.where(sel, jnp.exp(s - m_new), 0.0)
            alpha = jnp.exp(ms[h] - m_new)
            new_accs.append(alpha * accs[h] + _dot(p.astype(_CD), va))
            new_ms.append(m_new)
        return tuple(new_ms), tuple(new_accs), eq_seen

    init = (tuple(jnp.full((qb, 1), _NEG, _F32) for _ in range(A_HEADS)),
            tuple(jnp.zeros((qb, LANES), _F32) for _ in range(A_HEADS)),
            jnp.zeros((qb, 1), _F32))
    _, accs, _ = lax.fori_loop(0, nc, attn_body, init)

    outs = [acc / acc[:, HEAD_DIM:HEAD_DIM + 1] for acc in accs]
    for p in range(npair):
        blk = jnp.where(low, outs[2 * p], pltpu.roll(outs[2 * p + 1], HEAD_DIM, 1))
        o_ref[0, :, p * LANES:(p + 1) * LANES] = blk.astype(o_ref.dtype)


def _dsa_attention(qa, qi, wi, sa, sb, va, n_sel):
    b, s, _ = qa.shape
    qb, kc = QB_DSA, KC_DSA
    tri = jnp.triu(jnp.ones((kc, kc), _F32), k=1).astype(_CD)
    qspec = lambda w: pl.BlockSpec((1, qb, w), lambda bi, j: (bi, j, 0))
    kspec = pl.BlockSpec((1, s, LANES), lambda bi, j: (bi, 0, 0))
    return pl.pallas_call(
        functools.partial(_dsa_kernel, n_sel=n_sel, qb=qb, kc=kc),
        grid=(b, s // qb),
        in_specs=[qspec(qa.shape[2]), qspec(qi.shape[2]), qspec(wi.shape[2]),
                  kspec, kspec, kspec,
                  pl.BlockSpec((kc, kc), lambda bi, j: (0, 0))],
        out_specs=pl.BlockSpec((1, qb, A_HEADS * HEAD_DIM), lambda bi, j: (bi, j, 0)),
        out_shape=jax.ShapeDtypeStruct((b, s, A_HEADS * HEAD_DIM), _CD),
        scratch_shapes=[pltpu.VMEM((qb, s), jnp.int32)],
        compiler_params=_params("parallel", "arbitrary"),
        name="dsa_attention",
    )(qa, qi, wi, sa, sb, va, tri)


def _diff_kernel(lam_ref, g_ref, q_ref, k_ref, v_ref, o_ref, *, lam_init, tq):
    j = pl.program_id(1)
    lv = lam_ref[...]
    lam = (jnp.exp(jnp.sum(lv[0:1] * lv[1:2], axis=1, keepdims=True))
           - jnp.exp(jnp.sum(lv[2:3] * lv[3:4], axis=1, keepdims=True)) + lam_init)
    low = lax.broadcasted_iota(jnp.int32, (tq, LANES), 1) < HEAD_DIM
    zero = jnp.zeros((), _CD)
    row = lax.broadcasted_iota(jnp.int32, (2 * tq, tq), 0)
    col = lax.broadcasted_iota(jnp.int32, (2 * tq, tq), 1)
    causal = col <= jnp.where(row >= tq, row - tq, row)
    g = g_ref[...] * (1.0 - lam_init)

    for h in range(B_HEADS):
        hs = slice(h * LANES, (h + 1) * LANES)
        qs = q_ref[0, :, hs]
        qst = jnp.concatenate([jnp.where(low, qs, zero), jnp.where(low, zero, qs)], axis=0)

        def step(c, carry, masked, hs=hs, qst=qst):
            m, l, acc = carry
            off = pl.multiple_of(c * tq, tq)
            ks = k_ref[0, pl.ds(off, tq), hs]
            vs = v_ref[0, pl.ds(off, tq), hs]
            s = _dot_nt(qst, ks)
            if masked:
                s = jnp.where(causal, s, _NEG)
            m_new = jnp.maximum(m, jnp.max(s, axis=1, keepdims=True))
            p = jnp.exp(s - m_new)
            alpha = jnp.exp(m - m_new)
            l = alpha * l + jnp.sum(p, axis=1, keepdims=True)
            acc = alpha * acc + _dot(p.astype(_CD), vs)
            return m_new, l, acc

        init = (jnp.full((2 * tq, 1), _NEG, _F32), jnp.zeros((2 * tq, 1), _F32),
                jnp.zeros((2 * tq, LANES), _F32))
        carry = lax.fori_loop(0, j, functools.partial(step, masked=False), init)
        _, l, acc = step(j, carry, True)
        o = acc / l
        d = o[0:tq] - lam * o[tq:2 * tq]
        o_ref[0, :, hs] = _rms(d, g).astype(o_ref.dtype)


def _diff_attention(lams, g, q, k, v, lam_init):
    b, s, w = q.shape
    tq = TQ_DIFF
    full = pl.BlockSpec((1, s, w), lambda bi, j: (bi, 0, 0))
    return pl.pallas_call(
        functools.partial(_diff_kernel, lam_init=lam_init, tq=tq),
        grid=(b, s // tq),
        in_specs=[pl.BlockSpec(lams.shape, lambda bi, j: (0, 0)),
                  pl.BlockSpec(g.shape, lambda bi, j: (0, 0)),
                  pl.BlockSpec((1, tq, w), lambda bi, j: (bi, j, 0)),
                  full, full],
        out_specs=pl.BlockSpec((1, tq, w), lambda bi, j: (bi, j, 0)),
        out_shape=jax.ShapeDtypeStruct((b, s, w), _CD),
        compiler_params=_params("parallel", "arbitrary"),
        name="diff_attention",
    )(lams, g, q, k, v)


def _mem_kv_kernel(mem_ref, g_ref, w_ref, k_ref, v_ref):
    d = mem_ref.shape[2]
    mn = _rms(mem_ref[0], g_ref[...]).astype(_CD)
    k_ref[0] = _dot(mn, w_ref[:, 0:d]).astype(k_ref.dtype)
    v_ref[0] = _dot(mn, w_ref[:, d:2 * d]).astype(v_ref.dtype)


def _mem_kv(mem, g, w_kv):
    b, m, d = mem.shape
    blk = pl.BlockSpec((1, m, d), lambda bi: (bi, 0, 0))
    return pl.pallas_call(
        _mem_kv_kernel,
        grid=(b,),
        in_specs=[blk, pl.BlockSpec((1, d), lambda bi: (0, 0)),
                  pl.BlockSpec((d, 2 * d), lambda bi: (0, 0))],
        out_specs=[blk, blk],
        out_shape=[jax.ShapeDtypeStruct((b, m, d), _CD)] * 2,
        compiler_params=_params("parallel"),
        name="mem_kv",
    )(mem, g, w_kv)


def _mix_mem_kernel(h_ref, oa_ref, ob_ref, woa_ref, wob_ref, g_ref, wq_ref, mk_ref, mv_ref,
                    wo_ref, o_ref):
    h1 = h_ref[0] + _dot(oa_ref[0], woa_ref[...]) + _dot(ob_ref[0], wob_ref[...])
    xn = _rms(h1, g_ref[...]).astype(_CD)
    q = _dot(xn, wq_ref[...]).astype(_CD)
    hd = q.shape[1] // MEM_HEADS
    heads = []
    for hh in range(MEM_HEADS):
        hs = slice(hh * hd, (hh + 1) * hd)
        s = _dot_nt(q[:, hs], mk_ref[0, :, hs])
        p = jnp.exp(s - jnp.max(s, axis=1, keepdims=True))
        o = _dot(p.astype(_CD), mv_ref[0, :, hs]) / jnp.sum(p, axis=1, keepdims=True)
        heads.append(o.astype(_CD))
    o_ref[0] = h1 + _dot(jnp.concatenate(heads, axis=1), wo_ref[...])


def _mix_mem(h, oa, ob, woa, wob, g, wq, mk, mv, wo):
    b, s, d = h.shape
    tb = TB_MEM
    tok = lambda w: pl.BlockSpec((1, tb, w), lambda bi, j: (bi, j, 0))
    const = lambda a: pl.BlockSpec(a.shape, lambda bi, j: (0,) * a.ndim)
    memspec = pl.BlockSpec((1,) + mk.shape[1:], lambda bi, j: (bi, 0, 0))
    return pl.pallas_call(
        _mix_mem_kernel,
        grid=(b, s // tb),
        in_specs=[tok(d), tok(oa.shape[2]), tok(ob.shape[2]), const(woa), const(wob), const(g),
                  const(wq), memspec, memspec, const(wo)],
        out_specs=tok(d),
        out_shape=jax.ShapeDtypeStruct((b, s, d), _F32),
        compiler_params=_params("parallel", "parallel"),
        name="mix_mem",
    )(h, oa, ob, woa, wob, g, wq, mk, mv, wo)


def _ffn_kernel(h_ref, halo_ref, g_ref, wg_ref, wu_ref, cw_ref, cb_ref, wd_ref, fg_ref, o_ref, *,
                final_norm):
    j = pl.program_id(1)
    tb = h_ref.shape[1]
    g = g_ref[...]
    h = h_ref[0]
    xn = _rms(h, g).astype(_CD)
    xh = _rms(halo_ref[0], g).astype(_CD)
    gate = _dot(jnp.concatenate([xh, xn], axis=0), wg_ref[...])
    keep = jnp.where(j > 0, 1.0, 0.0)
    gate = jnp.concatenate([gate[0:SUBLANES] * keep, gate[SUBLANES:]], axis=0)
    cw = cw_ref[...]
    conv = cb_ref[...] + cw[CONV_WIDTH - 1:CONV_WIDTH] * gate[SUBLANES:SUBLANES + tb]
    for k in range(CONV_WIDTH - 1):
        sh = CONV_WIDTH - 1 - k
        conv = conv + cw[k:k + 1] * gate[SUBLANES - sh:SUBLANES - sh + tb]
    up = _dot(xn, wu_ref[...])
    act = (conv * jax.nn.sigmoid(conv) * up).astype(_CD)
    out = h + _dot(act, wd_ref[...])
    if final_norm:
        out = _rms(out, fg_ref[...])
    o_ref[0] = out


def _ffn(h, g, wg, wu, cw, cb, wd, fg, final_norm):
    b, s, d = h.shape
    tb = TB_FFN
    const = lambda a: pl.BlockSpec(a.shape, lambda bi, j: (0,) * a.ndim)
    per = tb // SUBLANES
    return pl.pallas_call(
        functools.partial(_ffn_kernel, final_norm=final_norm),
        grid=(b, s // tb),
        in_specs=[pl.BlockSpec((1, tb, d), lambda bi, j: (bi, j, 0)),
                  pl.BlockSpec((1, SUBLANES, d), lambda bi, j: (bi, jnp.maximum(j * per - 1, 0), 0)),
                  const(g), const(wg), const(wu), const(cw), const(cb), const(wd), const(fg)],
        out_specs=pl.BlockSpec((1, tb, d), lambda bi, j: (bi, j, 0)),
        out_shape=jax.ShapeDtypeStruct((b, s, d), _F32),
        compiler_params=_params("parallel", "parallel"),
        name="conv_glu",
    )(h, h, g, wg, wu, cw, cb, wd, fg)


def _prep_w_in(w):
    d = w.shape[0]
    o = 0
    parts = {}
    for name, width in (("qa", A_HEADS * HEAD_DIM), ("ka", HEAD_DIM), ("va", HEAD_DIM),
                        ("qi", IDX_HEADS * HEAD_DIM), ("ki", HEAD_DIM), ("wi", IDX_HEADS),
                        ("qb", 2 * B_HEADS * HEAD_DIM), ("kb", 2 * B_HEADS * HEAD_DIM),
                        ("vb", B_HEADS * B_V_DIM)):
        parts[name] = w[:, o:o + width]
        o += width
    scale = HEAD_DIM ** -0.5
    z = lambda n: jnp.zeros((d, n), w.dtype)
    cols = [parts["qa"] * scale, parts["qi"], parts["qb"] * scale, parts["kb"],
            parts["ka"], parts["ki"], parts["ki"], parts["ka"], parts["vb"],
            parts["va"], z(LANES - HEAD_DIM), parts["wi"], z(LANES - IDX_HEADS)]
    return jnp.concatenate(cols, axis=1).astype(_CD)


def kernel(x, mem, positions, mix_norm_g, w_in, lambda_q1, lambda_k1, lambda_q2, lambda_k2,
           diff_norm_g, w_out, mem_attn_norm_g, mem_norm_g, w_q_mem, w_kv_mem, w_o_mem, ffn_norm_g,
           w_gate_up, conv_w, conv_b, w_down, final_norm_g):
    b, s, d = x.shape
    depth = w_in.shape[0]
    d_ff = w_down.shape[1]
    n_sel = min(TOPK_MAX, s // 4)
    wa = A_HEADS * HEAD_DIM
    mem_scale = (d // MEM_HEADS) ** -0.5
    row = lambda v: v.reshape(1, -1).astype(_F32)

    rope = _rope_tables(positions)
    h = x
    for l in range(depth):
        qa, qi, qb, kb, sa, sb, vb, va, wi = _in_proj(
            h.reshape(b * s, d), row(mix_norm_g[l]), _prep_w_in(w_in[l]), rope)
        r3 = lambda t: t.reshape(b, s, t.shape[-1])
        out_a = _dsa_attention(r3(qa), r3(qi), r3(wi), r3(sa), r3(sb), r3(va), n_sel)
        lam_init = 0.8 - 0.6 * math.exp(-0.3 * l)
        lams = jnp.stack([lambda_q1[l], lambda_k1[l], lambda_q2[l], lambda_k2[l]]).astype(_F32)
        out_b = _diff_attention(lams, row(diff_norm_g[l]), r3(qb), r3(kb), r3(vb), lam_init)
        mk, mv = _mem_kv(mem, row(mem_norm_g[l]), w_kv_mem[l].astype(_CD))
        h = _mix_mem(h, out_a, out_b, w_out[l, :wa].astype(_CD), w_out[l, wa:].astype(_CD),
                     row(mem_attn_norm_g[l]), (w_q_mem[l] * mem_scale).astype(_CD), mk, mv,
                     w_o_mem[l].astype(_CD))
        h = _ffn(h, row(ffn_norm_g[l]), w_gate_up[l, :, :d_ff].astype(_CD),
                 w_gate_up[l, :, d_ff:].astype(_CD), conv_w[l].astype(_F32), row(conv_b[l]),
                 w_down[l].astype(_CD), row(final_norm_g), l == depth - 1)
    return h
```

```python
import functools
import math

import jax
import jax.numpy as jnp
from jax import lax
from jax.experimental import pallas as pl
from jax.experimental.pallas import tpu as pltpu

A_HEADS = 8
HEAD_DIM = 64
IDX_HEADS = 4
TOPK_MAX = 256
B_HEADS = 4
B_V_DIM = 128
MEM_HEADS = 4
CONV_WIDTH = 3
ROPE_THETA = 500000.0
ROT_DIM = 16
EPS = 1e-6

LANES = 128
SUBLANES = 8
VMEM_LIMIT = 56 * 1024 * 1024

_CD = jnp.bfloat16
_F32 = jnp.float32
_INT_MIN = -(2 ** 31)
_NEG = -1e30

C_QA, C_QI, C_QB, C_KB, C_SA, C_SB, C_VB, C_VA, C_WI, C_END = (
    0, 512, 768, 1280, 1792, 1920, 2048, 2560, 2688, 2816)

TB_PROJ = 512
QB_DSA = 128
KC_DSA = 512
TQ_DIFF = 512
TB_MEM = 512
TB_FFN = 256
COUNT_ROWS = 64


def _dot(a, b):
    return jnp.dot(a, b, preferred_element_type=_F32)


def _dot_nt(a, b):
    return lax.dot_general(a, b, (((1,), (1,)), ((), ())), preferred_element_type=_F32)


def _rms(x, g):
    return x * lax.rsqrt(jnp.mean(x * x, axis=-1, keepdims=True) + EPS) * g


def _params(*sem):
    return pltpu.CompilerParams(dimension_semantics=sem, vmem_limit_bytes=VMEM_LIMIT)


def _rope_kernel(pos_ref, freq_ref, o_ref):
    ang = pos_ref[...] * freq_ref[...]
    lane = lax.broadcasted_iota(jnp.int32, ang.shape, 1) & (HEAD_DIM - 1)
    c, s = jnp.cos(ang), jnp.sin(ang)
    half = ROT_DIM // 2
    o_ref[:, 0:LANES] = jnp.where(lane < ROT_DIM, c, 1.0)
    o_ref[:, LANES:2 * LANES] = jnp.where(lane < half, -s, 0.0)
    o_ref[:, 2 * LANES:3 * LANES] = jnp.where((lane >= half) & (lane < ROT_DIM), s, 0.0)


def _rope_tables(positions):
    n = positions.size
    tb = 1024
    half = ROT_DIM // 2
    inv_freq = ROPE_THETA ** (-jnp.arange(0, ROT_DIM, 2, dtype=_F32) / ROT_DIM)
    lane = jnp.arange(LANES) % HEAD_DIM
    freq = jnp.where(lane < ROT_DIM, inv_freq[lane % half], 0.0).astype(_F32)[None, :]
    pos = positions.astype(_F32).reshape(n, 1)
    return pl.pallas_call(
        _rope_kernel,
        grid=(n // tb,),
        in_specs=[pl.BlockSpec((tb, 1), lambda i: (i, 0)),
                  pl.BlockSpec((1, LANES), lambda i: (0, 0))],
        out_specs=pl.BlockSpec((tb, 3 * LANES), lambda i: (i, 0)),
        out_shape=jax.ShapeDtypeStruct((n, 3 * LANES), _F32),
        compiler_params=_params("parallel"),
        name="rope_tables",
    )(pos, freq)


def _in_proj_kernel(h_ref, g_ref, w_ref, rope_ref, qa_ref, qi_ref, qb_ref, kb_ref, sa_ref,
                    sb_ref, vbt_ref, vat_ref, wit_ref):
    xn = _rms(h_ref[...], g_ref[...]).astype(_CD)
    a = rope_ref[:, 0:LANES]
    bm = rope_ref[:, LANES:2 * LANES]
    bp = rope_ref[:, 2 * LANES:3 * LANES]

    def project(c0, c1, out_ref):
        p = _dot(xn, w_ref[:, c0:c1])
        for i in range((c1 - c0) // LANES):
            xb = p[:, i * LANES:(i + 1) * LANES]
            xb = (xb * a + pltpu.roll(xb, LANES - ROT_DIM // 2, 1) * bm
                  + pltpu.roll(xb, ROT_DIM // 2, 1) * bp)
            out_ref[:, i * LANES:(i + 1) * LANES] = xb.astype(out_ref.dtype)

    project(C_QA, C_QI, qa_ref)
    project(C_QI, C_QB, qi_ref)
    project(C_QB, C_KB, qb_ref)
    project(C_KB, C_SA, kb_ref)
    project(C_SA, C_SB, sa_ref)
    project(C_SB, C_VB, sb_ref)
    vbt_ref[...] = _dot(xn, w_ref[:, C_VB:C_VA]).T.astype(vbt_ref.dtype)
    pv = _dot(xn, w_ref[:, C_VA:C_WI])
    lane = lax.broadcasted_iota(jnp.int32, pv.shape, 1)
    vat_ref[...] = jnp.where(lane == HEAD_DIM, 1.0, pv).T.astype(vat_ref.dtype)
    wit_ref[...] = _dot(xn, w_ref[:, C_WI:C_END]).T[0:SUBLANES]


def _in_proj(h2d, g, w, rope):
    n, d = h2d.shape
    tb = TB_PROJ
    widths = (C_QI - C_QA, C_QB - C_QI, C_KB - C_QB, C_SA - C_KB, C_SB - C_SA, C_VB - C_SB)
    out_shape = [jax.ShapeDtypeStruct((n, wd), _CD) for wd in widths]
    out_specs = [pl.BlockSpec((tb, s.shape[1]), lambda i: (i, 0)) for s in out_shape]
    for rows, dt in ((C_VA - C_VB, _CD), (C_WI - C_VA, _CD), (SUBLANES, _F32)):
        out_shape.append(jax.ShapeDtypeStruct((rows, n), dt))
        out_specs.append(pl.BlockSpec((rows, tb), lambda i: (0, i)))
    return pl.pallas_call(
        _in_proj_kernel,
        grid=(n // tb,),
        in_specs=[pl.BlockSpec((tb, d), lambda i: (i, 0)),
                  pl.BlockSpec((1, d), lambda i: (0, 0)),
                  pl.BlockSpec((d, C_END), lambda i: (0, 0)),
                  pl.BlockSpec((tb, 3 * LANES), lambda i: (i, 0))],
        out_specs=out_specs,
        out_shape=out_shape,
        compiler_params=_params("parallel"),
        name="in_proj",
    )(h2d, g, w, rope)


def _dsa_kernel(qa_ref, qi_ref, wit_ref, sa_ref, sb_ref, vat_ref, tri_ref, o_ref,
                keys_ref, acc_ref, bias_ref, eqseen_ref, *, n_sel, qb, kc):
    j = pl.program_id(1)
    q0 = j * qb
    nc = (q0 + qb + kc - 1) // kc
    low = lax.broadcasted_iota(jnp.int32, (qb, LANES), 1) < HEAD_DIM
    zero = jnp.zeros((), _CD)

    def lo(x):
        return jnp.where(low, x, zero)

    def hi(x):
        return jnp.where(low, zero, x)

    qi = qi_ref[0]
    qi_even = jnp.concatenate([lo(qi[:, 0:LANES]), lo(qi[:, LANES:2 * LANES])], axis=0)
    qi_odd = jnp.concatenate([hi(qi[:, 0:LANES]), hi(qi[:, LANES:2 * LANES])], axis=0)
    wi = wit_ref[...]
    row_minus_col = (lax.broadcasted_iota(jnp.int32, (kc, qb), 0)
                     - lax.broadcasted_iota(jnp.int32, (kc, qb), 1))

    def score_body(c, carry):
        off = pl.multiple_of(c * kc, kc)
        sa = sa_ref[0, pl.ds(off, kc), :]
        sb = sb_ref[0, pl.ds(off, kc), :]
        de = jnp.maximum(_dot_nt(sb, qi_even), 0.0)
        do = jnp.maximum(_dot_nt(sa, qi_odd), 0.0)
        sc = (de[:, 0:qb] * wi[0:1] + do[:, 0:qb] * wi[1:2]
              + de[:, qb:2 * qb] * wi[2:3] + do[:, qb:2 * qb] * wi[3:4])
        bits = lax.bitcast_convert_type(sc, jnp.int32)
        key = bits ^ ((bits >> 31) & 0x7FFFFFFF)
        key = jnp.where(row_minus_col <= q0 - off, key, _INT_MIN)
        keys_ref[pl.ds(off, kc), :] = key
        return carry

    lax.fori_loop(0, nc, score_body, 0)

    def count_ge(cand):
        def body(c, acc):
            off = pl.multiple_of(c * kc, kc)
            ge = jnp.where(keys_ref[pl.ds(off, kc), :] >= cand, 1.0, 0.0)
            return acc + jnp.sum(ge.reshape(kc // COUNT_ROWS, COUNT_ROWS, qb), axis=0)
        acc = lax.fori_loop(0, nc, body, jnp.zeros((COUNT_ROWS, qb), _F32))
        return jnp.sum(acc, axis=0, keepdims=True)

    def search_body(it, carry):
        cur, n_cur = carry
        cand = cur + jnp.left_shift(jnp.int32(1), 31 - it)
        n = count_ge(cand)
        ok = n >= n_sel
        return jnp.where(ok, cand, cur), jnp.where(ok, n, n_cur)

    thr, n_ge = lax.fori_loop(
        0, 32, search_body,
        (jnp.full((1, qb), _INT_MIN, jnp.int32), jnp.full((1, qb), float(n_sel), _F32)))
    floor_row = thr == _INT_MIN
    need = jnp.where(floor_row, 0.0, n_sel - count_ge(thr + 1))
    has_ties = jnp.max(jnp.where((n_ge > n_sel) & jnp.logical_not(floor_row), 1.0, 0.0)) > 0.0
    thr_all_equal = jnp.where(floor_row, _INT_MIN + 1, thr)

    qa = qa_ref[0]
    npair = A_HEADS // 2
    qa_even = jnp.concatenate([lo(qa[:, p * LANES:(p + 1) * LANES]) for p in range(npair)], axis=0)
    qa_odd = jnp.concatenate([hi(qa[:, p * LANES:(p + 1) * LANES]) for p in range(npair)], axis=0)
    acc_ref[...] = jnp.zeros_like(acc_ref)
    eqseen_ref[...] = jnp.zeros_like(eqseen_ref)

    def attn_body(c, m):
        off = pl.multiple_of(c * kc, kc)
        key = keys_ref[pl.ds(off, kc), :]

        @pl.when(has_ties)
        def _():
            eq = key == thr
            eqf = jnp.where(eq, 1.0, 0.0)
            rank = _dot(tri_ref[...], eqf.astype(_CD)) + eqseen_ref[...]
            sel = (key > thr) | (eq & (rank < need))
            bias_ref[...] = jnp.where(sel, 0.0, _NEG)
            eqseen_ref[...] += jnp.sum(eqf, axis=0, keepdims=True)

        @pl.when(jnp.logical_not(has_ties))
        def _():
            bias_ref[...] = jnp.where(key >= thr_all_equal, 0.0, _NEG)

        bias = bias_ref[...]
        s_even = _dot_nt(sa_ref[0, pl.ds(off, kc), :], qa_even)
        s_odd = _dot_nt(sb_ref[0, pl.ds(off, kc), :], qa_odd)
        ps, m_news = [], []
        for blk in range(A_HEADS):
            src = s_even if blk < npair else s_odd
            s = src[:, (blk % npair) * qb:(blk % npair + 1) * qb] + bias
            m_new = jnp.maximum(m[:, blk * qb:(blk + 1) * qb], jnp.max(s, axis=0, keepdims=True))
            ps.append(jnp.exp(s - m_new).astype(_CD))
            m_news.append(m_new)
        m_new = jnp.concatenate(m_news, axis=1)
        alpha = jnp.exp(m - m_new)
        pv = _dot(vat_ref[:, pl.ds(off, kc)], jnp.concatenate(ps, axis=1))
        acc_ref[...] = alpha * acc_ref[...] + pv
        return m_new

    lax.fori_loop(0, nc, attn_body, jnp.full((1, A_HEADS * qb), _NEG, _F32))

    acc = acc_ref[...]
    out_t = acc / acc[HEAD_DIM:HEAD_DIM + 1, :]
    for p in range(npair):
        even = out_t[:, p * qb:(p + 1) * qb].T
        odd = out_t[:, (npair + p) * qb:(npair + p + 1) * qb].T
        blk = jnp.where(low, even, pltpu.roll(odd, HEAD_DIM, 1))
        o_ref[0, :, p * LANES:(p + 1) * LANES] = blk.astype(o_ref.dtype)


def _dsa_attention(qa, qi, wit, sa, sb, vat, n_sel):
    b, s, _ = qa.shape
    qb, kc = QB_DSA, KC_DSA
    nq = s // qb
    tri = jnp.tril(jnp.ones((kc, kc), _F32), k=-1).astype(_CD)
    qspec = lambda w: pl.BlockSpec((1, qb, w), lambda bi, j: (bi, j, 0))
    kspec = pl.BlockSpec((1, s, LANES), lambda bi, j: (bi, 0, 0))
    return pl.pallas_call(
        functools.partial(_dsa_kernel, n_sel=n_sel, qb=qb, kc=kc),
        grid=(b, nq),
        in_specs=[qspec(qa.shape[2]), qspec(qi.shape[2]),
                  pl.BlockSpec((SUBLANES, qb), lambda bi, j: (0, bi * nq + j)),
                  kspec, kspec,
                  pl.BlockSpec((LANES, s), lambda bi, j: (0, bi)),
                  pl.BlockSpec((kc, kc), lambda bi, j: (0, 0))],
        out_specs=pl.BlockSpec((1, qb, A_HEADS * HEAD_DIM), lambda bi, j: (bi, j, 0)),
        out_shape=jax.ShapeDtypeStruct((b, s, A_HEADS * HEAD_DIM), _CD),
        scratch_shapes=[pltpu.VMEM((s, qb), jnp.int32),
                        pltpu.VMEM((LANES, A_HEADS * qb), _F32),
                        pltpu.VMEM((kc, qb), _F32),
                        pltpu.VMEM((1, qb), _F32)],
        compiler_params=_params("parallel", "arbitrary"),
        name="dsa_attention",
    )(qa, qi, wit, sa, sb, vat, tri)


def _diff_kernel(lam_ref, g_ref, q_ref, k_ref, vt_ref, o_ref, acc_ref, *, lam_init, tq):
    j = pl.program_id(1)
    lv = lam_ref[...]
    lam = (jnp.exp(jnp.sum(lv[0:1] * lv[1:2], axis=1, keepdims=True))
           - jnp.exp(jnp.sum(lv[2:3] * lv[3:4], axis=1, keepdims=True)) + lam_init)
    low = lax.broadcasted_iota(jnp.int32, (tq, LANES), 1) < HEAD_DIM
    zero = jnp.zeros((), _CD)
    krow = lax.broadcasted_iota(jnp.int32, (tq, 2 * tq), 0)
    qcol = lax.broadcasted_iota(jnp.int32, (tq, 2 * tq), 1)
    causal = krow <= jnp.where(qcol >= tq, qcol - tq, qcol)
    g = g_ref[...] * (1.0 - lam_init)

    for h in range(B_HEADS):
        hs = slice(h * LANES, (h + 1) * LANES)
        qs = q_ref[0, :, hs]
        qst = jnp.concatenate([jnp.where(low, qs, zero), jnp.where(low, zero, qs)], axis=0)
        acc_ref[...] = jnp.zeros_like(acc_ref)

        def step(c, carry, masked, hs=hs, qst=qst):
            m, l = carry
            off = pl.multiple_of(c * tq, tq)
            s = _dot_nt(k_ref[0, pl.ds(off, tq), hs], qst)
            if masked:
                s = jnp.where(causal, s, _NEG)
            m_new = jnp.maximum(m, jnp.max(s, axis=0, keepdims=True))
            p = jnp.exp(s - m_new)
            alpha = jnp.exp(m - m_new)
            l = alpha * l + jnp.sum(p, axis=0, keepdims=True)
            acc_ref[...] = alpha * acc_ref[...] + _dot(vt_ref[hs, pl.ds(off, tq)], p.astype(_CD))
            return m_new, l

        init = (jnp.full((1, 2 * tq), _NEG, _F32), jnp.zeros((1, 2 * tq), _F32))
        carry = lax.fori_loop(0, j, functools.partial(step, masked=False), init)
        _, l = step(j, carry, True)
        o_t = acc_ref[...] / l
        d = (o_t[:, 0:tq] - lam * o_t[:, tq:2 * tq]).T
        o_ref[0, :, hs] = _rms(d, g).astype(o_ref.dtype)


def _diff_attention(lams, g, q, k, vt, lam_init):
    b, s, w = q.shape
    tq = TQ_DIFF
    full = pl.BlockSpec((1, s, w), lambda bi, j: (bi, 0, 0))
    return pl.pallas_call(
        functools.partial(_diff_kernel, lam_init=lam_init, tq=tq),
        grid=(b, s // tq),
        in_specs=[pl.BlockSpec(lams.shape, lambda bi, j: (0, 0)),
                  pl.BlockSpec(g.shape, lambda bi, j: (0, 0)),
                  pl.BlockSpec((1, tq, w), lambda bi, j: (bi, j, 0)),
                  full,
                  pl.BlockSpec((w, s), lambda bi, j: (0, bi))],
        out_specs=pl.BlockSpec((1, tq, w), lambda bi, j: (bi, j, 0)),
        out_shape=jax.ShapeDtypeStruct((b, s, w), _CD),
        scratch_shapes=[pltpu.VMEM((LANES, 2 * tq), _F32)],
        compiler_params=_params("parallel", "arbitrary"),
        name="diff_attention",
    )(lams, g, q, k, vt)


def _mem_kv_kernel(mem_ref, g_ref, w_ref, k_ref, v_ref):
    d = mem_ref.shape[2]
    mn = _rms(mem_ref[0], g_ref[...]).astype(_CD)
    k_ref[0] = _dot(mn, w_ref[:, 0:d]).astype(k_ref.dtype)
    v_ref[0] = _dot(mn, w_ref[:, d:2 * d]).astype(v_ref.dtype)


def _mem_kv(mem, g, w_kv):
    b, m, d = mem.shape
    blk = pl.BlockSpec((1, m, d), lambda bi: (bi, 0, 0))
    return pl.pallas_call(
        _mem_kv_kernel,
        grid=(b,),
        in_specs=[blk, pl.BlockSpec((1, d), lambda bi: (0, 0)),
                  pl.BlockSpec((d, 2 * d), lambda bi: (0, 0))],
        out_specs=[blk, blk],
        out_shape=[jax.ShapeDtypeStruct((b, m, d), _CD)] * 2,
        compiler_params=_params("parallel"),
        name="mem_kv",
    )(mem, g, w_kv)


def _mix_mem_kernel(h_ref, oa_ref, ob_ref, woa_ref, wob_ref, g_ref, wq_ref, mk_ref, mv_ref,
                    wo_ref, o_ref):
    h1 = h_ref[0] + _dot(oa_ref[0], woa_ref[...]) + _dot(ob_ref[0], wob_ref[...])
    xn = _rms(h1, g_ref[...]).astype(_CD)
    q = _dot(xn, wq_ref[...]).astype(_CD)
    hd = q.shape[1] // MEM_HEADS
    heads = []
    for hh in range(MEM_HEADS):
        hs = slice(hh * hd, (hh + 1) * hd)
        s = _dot_nt(q[:, hs], mk_ref[0, :, hs])
        p = jnp.exp(s - jnp.max(s, axis=1, keepdims=True))
        o = _dot(p.astype(_CD), mv_ref[0, :, hs]) / jnp.sum(p, axis=1, keepdims=True)
        heads.append(o.astype(_CD))
    o_ref[0] = h1 + _dot(jnp.concatenate(heads, axis=1), wo_ref[...])


def _mix_mem(h, oa, ob, woa, wob, g, wq, mk, mv, wo):
    b, s, d = h.shape
    tb = TB_MEM
    tok = lambda w: pl.BlockSpec((1, tb, w), lambda bi, j: (bi, j, 0))
    const = lambda a: pl.BlockSpec(a.shape, lambda bi, j: (0,) * a.ndim)
    memspec = pl.BlockSpec((1,) + mk.shape[1:], lambda bi, j: (bi, 0, 0))
    return pl.pallas_call(
        _mix_mem_kernel,
        grid=(b, s // tb),
        in_specs=[tok(d), tok(oa.shape[2]), tok(ob.shape[2]), const(woa), const(wob), const(g),
                  const(wq), memspec, memspec, const(wo)],
        out_specs=tok(d),
        out_shape=jax.ShapeDtypeStruct((b, s, d), _F32),
        compiler_params=_params("parallel", "parallel"),
        name="mix_mem",
    )(h, oa, ob, woa, wob, g, wq, mk, mv, wo)


def _ffn_kernel(h_ref, halo_ref, g_ref, wg_ref, wu_ref, cw_ref, cb_ref, wd_ref, fg_ref, o_ref, *,
                final_norm):
    j = pl.program_id(1)
    tb = h_ref.shape[1]
    g = g_ref[...]
    h = h_ref[0]
    xn = _rms(h, g).astype(_CD)
    xh = _rms(halo_ref[0], g).astype(_CD)
    gate = _dot(jnp.concatenate([xh, xn], axis=0), wg_ref[...])
    keep = jnp.where(j > 0, 1.0, 0.0)
    gate = jnp.concatenate([gate[0:SUBLANES] * keep, gate[SUBLANES:]], axis=0)
    cw = cw_ref[...]
    conv = cb_ref[...] + cw[CONV_WIDTH - 1:CONV_WIDTH] * gate[SUBLANES:SUBLANES + tb]
    for k in range(CONV_WIDTH - 1):
        sh = CONV_WIDTH - 1 - k
        conv = conv + cw[k:k + 1] * gate[SUBLANES - sh:SUBLANES - sh + tb]
    up = _dot(xn, wu_ref[...])
    act = (conv * jax.nn.sigmoid(conv) * up).astype(_CD)
    out = h + _dot(act, wd_ref[...])
    if final_norm:
        out = _rms(out, fg_ref[...])
    o_ref[0] = out


def _ffn(h, g, wg, wu, cw, cb, wd, fg, final_norm):
    b, s, d = h.shape
    tb = TB_FFN
    const = lambda a: pl.BlockSpec(a.shape, lambda bi, j: (0,) * a.ndim)
    per = tb // SUBLANES
    return pl.pallas_call(
        functools.partial(_ffn_kernel, final_norm=final_norm),
        grid=(b, s // tb),
        in_specs=[pl.BlockSpec((1, tb, d), lambda bi, j: (bi, j, 0)),
                  pl.BlockSpec((1, SUBLANES, d), lambda bi, j: (bi, jnp.maximum(j * per - 1, 0), 0)),
                  const(g), const(wg), const(wu), const(cw), const(cb), const(wd), const(fg)],
        out_specs=pl.BlockSpec((1, tb, d), lambda bi, j: (bi, j, 0)),
        out_shape=jax.ShapeDtypeStruct((b, s, d), _F32),
        compiler_params=_params("parallel", "parallel"),
        name="conv_glu",
    )(h, h, g, wg, wu, cw, cb, wd, fg)


def _prep_w_in(w):
    d = w.shape[0]
    o = 0
    parts = {}
    for name, width in (("qa", A_HEADS * HEAD_DIM), ("ka", HEAD_DIM), ("va", HEAD_DIM),
                        ("qi", IDX_HEADS * HEAD_DIM), ("ki", HEAD_DIM), ("wi", IDX_HEADS),
                        ("qb", 2 * B_HEADS * HEAD_DIM), ("kb", 2 * B_HEADS * HEAD_DIM),
                        ("vb", B_HEADS * B_V_DIM)):
        parts[name] = w[:, o:o + width]
        o += width
    scale = HEAD_DIM ** -0.5
    z = lambda n: jnp.zeros((d, n), w.dtype)
    cols = [parts["qa"] * scale, parts["qi"], parts["qb"] * scale, parts["kb"],
            parts["ka"], parts["ki"], parts["ki"], parts["ka"], parts["vb"],
            parts["va"], z(LANES - HEAD_DIM),
            parts["wi"] * (HEAD_DIM ** -0.5 * IDX_HEADS ** -0.5), z(LANES - IDX_HEADS)]
    return jnp.concatenate(cols, axis=1).astype(_CD)


def kernel(x, mem, positions, mix_norm_g, w_in, lambda_q1, lambda_k1, lambda_q2, lambda_k2,
           diff_norm_g, w_out, mem_attn_norm_g, mem_norm_g, w_q_mem, w_kv_mem, w_o_mem, ffn_norm_g,
           w_gate_up, conv_w, conv_b, w_down, final_norm_g):
    b, s, d = x.shape
    depth = w_in.shape[0]
    d_ff = w_down.shape[1]
    n_sel = min(TOPK_MAX, s // 4)
    wa = A_HEADS * HEAD_DIM
    mem_scale = (d // MEM_HEADS) ** -0.5
    row = lambda v: v.reshape(1, -1).astype(_F32)

    rope = _rope_tables(positions)
    h = x
    for l in range(depth):
        qa, qi, qb, kb, sa, sb, vbt, vat, wit = _in_proj(
            h.reshape(b * s, d), row(mix_norm_g[l]), _prep_w_in(w_in[l]), rope)
        r3 = lambda t: t.reshape(b, s, t.shape[-1])
        out_a = _dsa_attention(r3(qa), r3(qi), wit, r3(sa), r3(sb), vat, n_sel)
        lam_init = 0.8 - 0.6 * math.exp(-0.3 * l)
        lams = jnp.stack([lambda_q1[l], lambda_k1[l], lambda_q2[l], lambda_k2[l]]).astype(_F32)
        out_b = _diff_attention(lams, row(diff_norm_g[l]), r3(qb), r3(kb), vbt, lam_init)
        mk, mv = _mem_kv(mem, row(mem_norm_g[l]), w_kv_mem[l].astype(_CD))
        h = _mix_mem(h, out_a, out_b, w_out[l, :wa].astype(_CD), w_out[l, wa:].astype(_CD),
                     row(mem_attn_norm_g[l]), (w_q_mem[l] * mem_scale).astype(_CD), mk, mv,
                     w_o_mem[l].astype(_CD))
        h = _ffn(h, row(ffn_norm_g[l]), w_gate_up[l, :, :d_ff].astype(_CD),
                 w_gate_up[l, :, d_ff:].astype(_CD), conv_w[l].astype(_F32), row(conv_b[l]),
                 w_down[l].astype(_CD), row(final_norm_g), l == depth - 1)
    return h
```

```python
import functools
import math

import jax
import jax.numpy as jnp
from jax import lax
from jax.experimental import pallas as pl
from jax.experimental.pallas import tpu as pltpu

A_HEADS = 8
HEAD_DIM = 64
IDX_HEADS = 4
TOPK_MAX = 256
B_HEADS = 4
B_V_DIM = 128
MEM_HEADS = 4
CONV_WIDTH = 3
ROPE_THETA = 500000.0
ROT_DIM = 16
EPS = 1e-6

LANES = 128
SUBLANES = 8
VMEM_LIMIT = 56 * 1024 * 1024

_CD = jnp.bfloat16
_F32 = jnp.float32
_INT_MIN = -(2 ** 31)
_NEG = -1e30

C_QA, C_QI, C_QB, C_KB, C_SA, C_SB, C_VB, C_VA, C_WI, C_END = (
    0, 512, 768, 1280, 1792, 1920, 2048, 2560, 2688, 2816)

TB_PROJ = 512
QB_DSA = 128
KC_DSA = 512
TQ_DIFF = 512
TB_MEM = 512
TB_FFN = 256
COUNT_ROWS = 64


def _dot(a, b):
    return jnp.dot(a, b, preferred_element_type=_F32)


def _dot_nt(a, b):
    return lax.dot_general(a, b, (((1,), (1,)), ((), ())), preferred_element_type=_F32)


def _rms(x, g):
    return x * lax.rsqrt(jnp.mean(x * x, axis=-1, keepdims=True) + EPS) * g


def _params(*sem):
    return pltpu.CompilerParams(dimension_semantics=sem, vmem_limit_bytes=VMEM_LIMIT)


def _rope_kernel(pos_ref, freq_ref, o_ref):
    ang = pos_ref[...] * freq_ref[...]
    lane = lax.broadcasted_iota(jnp.int32, ang.shape, 1) & (HEAD_DIM - 1)
    c, s = jnp.cos(ang), jnp.sin(ang)
    half = ROT_DIM // 2
    o_ref[:, 0:LANES] = jnp.where(lane < ROT_DIM, c, 1.0)
    o_ref[:, LANES:2 * LANES] = jnp.where(lane < half, -s, 0.0)
    o_ref[:, 2 * LANES:3 * LANES] = jnp.where((lane >= half) & (lane < ROT_DIM), s, 0.0)


def _rope_tables(positions):
    n = positions.size
    tb = 1024
    half = ROT_DIM // 2
    inv_freq = ROPE_THETA ** (-jnp.arange(0, ROT_DIM, 2, dtype=_F32) / ROT_DIM)
    lane = jnp.arange(LANES) % HEAD_DIM
    freq = jnp.where(lane < ROT_DIM, inv_freq[lane % half], 0.0).astype(_F32)[None, :]
    pos = positions.astype(_F32).reshape(n, 1)
    return pl.pallas_call(
        _rope_kernel,
        grid=(n // tb,),
        in_specs=[pl.BlockSpec((tb, 1), lambda i: (i, 0)),
                  pl.BlockSpec((1, LANES), lambda i: (0, 0))],
        out_specs=pl.BlockSpec((tb, 3 * LANES), lambda i: (i, 0)),
        out_shape=jax.ShapeDtypeStruct((n, 3 * LANES), _F32),
        compiler_params=_params("parallel"),
        name="rope_tables",
    )(pos, freq)


def _in_proj_kernel(h_ref, g_ref, w_ref, rope_ref, qa_ref, qi_ref, qb_ref, kb_ref, sa_ref,
                    sb_ref, vbt_ref, vat_ref, wit_ref):
    xn = _rms(h_ref[...], g_ref[...]).astype(_CD)
    a = rope_ref[:, 0:LANES]
    bm = rope_ref[:, LANES:2 * LANES]
    bp = rope_ref[:, 2 * LANES:3 * LANES]

    def project(c0, c1, out_ref):
        p = _dot(xn, w_ref[:, c0:c1])
        for i in range((c1 - c0) // LANES):
            xb = p[:, i * LANES:(i + 1) * LANES]
            xb = (xb * a + pltpu.roll(xb, LANES - ROT_DIM // 2, 1) * bm
                  + pltpu.roll(xb, ROT_DIM // 2, 1) * bp)
            out_ref[:, i * LANES:(i + 1) * LANES] = xb.astype(out_ref.dtype)

    project(C_QA, C_QI, qa_ref)
    project(C_QI, C_QB, qi_ref)
    project(C_QB, C_KB, qb_ref)
    project(C_KB, C_SA, kb_ref)
    project(C_SA, C_SB, sa_ref)
    project(C_SB, C_VB, sb_ref)
    vbt_ref[...] = _dot(xn, w_ref[:, C_VB:C_VA]).T.astype(vbt_ref.dtype)
    pv = _dot(xn, w_ref[:, C_VA:C_WI])
    lane = lax.broadcasted_iota(jnp.int32, pv.shape, 1)
    vat_ref[...] = jnp.where(lane == HEAD_DIM, 1.0, pv).T.astype(vat_ref.dtype)
    wit_ref[...] = _dot(xn, w_ref[:, C_WI:C_END]).T[0:SUBLANES]


def _in_proj(h2d, g, w, rope):
    n, d = h2d.shape
    tb = TB_PROJ
    widths = (C_QI - C_QA, C_QB - C_QI, C_KB - C_QB, C_SA - C_KB, C_SB - C_SA, C_VB - C_SB)
    out_shape = [jax.ShapeDtypeStruct((n, wd), _CD) for wd in widths]
    out_specs = [pl.BlockSpec((tb, s.shape[1]), lambda i: (i, 0)) for s in out_shape]
    for rows, dt in ((C_VA - C_VB, _CD), (C_WI - C_VA, _CD), (SUBLANES, _F32)):
        out_shape.append(jax.ShapeDtypeStruct((rows, n), dt))
        out_specs.append(pl.BlockSpec((rows, tb), lambda i: (0, i)))
    return pl.pallas_call(
        _in_proj_kernel,
        grid=(n // tb,),
        in_specs=[pl.BlockSpec((tb, d), lambda i: (i, 0)),
                  pl.BlockSpec((1, d), lambda i: (0, 0)),
                  pl.BlockSpec((d, C_END), lambda i: (0, 0)),
                  pl.BlockSpec((tb, 3 * LANES), lambda i: (i, 0))],
        out_specs=out_specs,
        out_shape=out_shape,
        compiler_params=_params("parallel"),
        name="in_proj",
    )(h2d, g, w, rope)


def _dsa_kernel(qa_ref, qi_ref, wit_ref, sa_ref, sb_ref, vat_ref, tri_ref, o_ref,
                keys_ref, acc_ref, sa_stage_ref, sb_stage_ref, *, n_sel, qb, kc):
    j = pl.program_id(1)
    q0 = j * qb
    nc = (q0 + qb + kc - 1) // kc
    low = lax.broadcasted_iota(jnp.int32, (qb, LANES), 1) < HEAD_DIM
    zero = jnp.zeros((), _CD)

    def lo(x):
        return jnp.where(low, x, zero)

    def hi(x):
        return jnp.where(low, zero, x)

    qi = qi_ref[0]
    qi_even = jnp.concatenate([lo(qi[:, 0:LANES]), lo(qi[:, LANES:2 * LANES])], axis=0)
    qi_odd = jnp.concatenate([hi(qi[:, 0:LANES]), hi(qi[:, LANES:2 * LANES])], axis=0)
    wi = wit_ref[...]
    row_minus_col = (lax.broadcasted_iota(jnp.int32, (kc, qb), 0)
                     - lax.broadcasted_iota(jnp.int32, (kc, qb), 1))

    def score_body(c, carry):
        off = pl.multiple_of(c * kc, kc)
        sa = sa_ref[0, pl.ds(off, kc), :]
        sb = sb_ref[0, pl.ds(off, kc), :]
        de = jnp.maximum(_dot_nt(sb, qi_even), 0.0)
        do = jnp.maximum(_dot_nt(sa, qi_odd), 0.0)
        sc = (de[:, 0:qb] * wi[0:1] + do[:, 0:qb] * wi[1:2]
              + de[:, qb:2 * qb] * wi[2:3] + do[:, qb:2 * qb] * wi[3:4])
        bits = lax.bitcast_convert_type(sc, jnp.int32)
        key = bits ^ ((bits >> 31) & 0x7FFFFFFF)
        key = jnp.where(row_minus_col <= q0 - off, key, _INT_MIN)
        keys_ref[pl.ds(off, kc), :] = key
        return carry

    lax.fori_loop(0, nc, score_body, 0)

    def count_ge(cand):
        def body(c, acc):
            off = pl.multiple_of(c * kc, kc)
            ge = jnp.where(keys_ref[pl.ds(off, kc), :] >= cand, 1.0, 0.0)
            return acc + jnp.sum(ge.reshape(kc // COUNT_ROWS, COUNT_ROWS, qb), axis=0)
        acc = lax.fori_loop(0, nc, body, jnp.zeros((COUNT_ROWS, qb), _F32))
        return jnp.sum(acc, axis=0, keepdims=True)

    def search_body(it, carry):
        cur, n_cur = carry
        cand = cur + jnp.left_shift(jnp.int32(1), 31 - it)
        n = count_ge(cand)
        ok = n >= n_sel
        return jnp.where(ok, cand, cur), jnp.where(ok, n, n_cur)

    thr, n_ge = lax.fori_loop(
        0, 32, search_body,
        (jnp.full((1, qb), _INT_MIN, jnp.int32), jnp.full((1, qb), float(n_sel), _F32)))
    floor_row = thr == _INT_MIN
    need = jnp.where(floor_row, 0.0, n_sel - count_ge(thr + 1))
    has_ties = jnp.max(jnp.where((n_ge > n_sel) & jnp.logical_not(floor_row), 1.0, 0.0)) > 0.0
    thr_all_equal = jnp.where(floor_row, _INT_MIN + 1, thr)

    qa = qa_ref[0]
    npair = A_HEADS // 2
    qa_even = jnp.concatenate([lo(qa[:, p * LANES:(p + 1) * LANES]) for p in range(npair)], axis=0)
    qa_odd = jnp.concatenate([hi(qa[:, p * LANES:(p + 1) * LANES]) for p in range(npair)], axis=0)
    acc_ref[...] = jnp.zeros_like(acc_ref)

    @pl.when(has_ties)
    def _():
        def retire_body(c, eq_seen):
            off = pl.multiple_of(c * kc, kc)
            key = keys_ref[pl.ds(off, kc), :]
            eq = key == thr
            eqf = jnp.where(eq, 1.0, 0.0)
            rank = _dot(tri_ref[...], eqf.astype(_CD)) + eq_seen
            keys_ref[pl.ds(off, kc), :] = jnp.where(eq & (rank >= need), _INT_MIN, key)
            return eq_seen + jnp.sum(eqf, axis=0, keepdims=True)
        lax.fori_loop(0, nc, retire_body, jnp.zeros((1, qb), _F32))

    hk = kc // 2
    s_total = keys_ref.shape[0]

    def qk(row0, s_ref):
        s_ref[:, 0:npair * qb] = _dot_nt(sa_ref[0, pl.ds(row0, hk), :], qa_even)
        s_ref[:, npair * qb:] = _dot_nt(sb_ref[0, pl.ds(row0, hk), :], qa_odd)

    def consume(row0, s_ref, m):
        bias = jnp.where(keys_ref[pl.ds(row0, hk), :] >= thr_all_equal, 0.0, _NEG)
        ps, m_news = [], []
        for blk in range(A_HEADS):
            cs = slice(blk * qb, (blk + 1) * qb)
            s = s_ref[:, cs] + bias
            m_new = jnp.maximum(m[:, cs], jnp.max(s, axis=0, keepdims=True))
            ps.append(jnp.exp2(s - m_new).astype(_CD))
            m_news.append(m_new)
        m_new = jnp.concatenate(m_news, axis=1)
        pv = _dot(vat_ref[:, pl.ds(row0, hk)], jnp.concatenate(ps, axis=1))
        acc_ref[...] = jnp.exp2(m - m_new) * acc_ref[...] + pv
        return m_new

    qk(0, sa_stage_ref)

    def attn_body(c, m):
        off = pl.multiple_of(c * kc, kc)
        qk(off + hk, sb_stage_ref)
        m = consume(off, sa_stage_ref, m)
        qk(pl.multiple_of(jnp.minimum(off + kc, s_total - hk), hk), sa_stage_ref)
        return consume(off + hk, sb_stage_ref, m)

    lax.fori_loop(0, nc, attn_body, jnp.full((1, A_HEADS * qb), _NEG, _F32))

    acc = acc_ref[...]
    out_t = acc / acc[HEAD_DIM:HEAD_DIM + 1, :]
    for p in range(npair):
        even = out_t[:, p * qb:(p + 1) * qb].T
        odd = out_t[:, (npair + p) * qb:(npair + p + 1) * qb].T
        blk = jnp.where(low, even, pltpu.roll(odd, HEAD_DIM, 1))
        o_ref[0, :, p * LANES:(p + 1) * LANES] = blk.astype(o_ref.dtype)


def _dsa_attention(qa, qi, wit, sa, sb, vat, n_sel):
    b, s, _ = qa.shape
    qb, kc = QB_DSA, KC_DSA
    nq = s // qb
    tri = jnp.tril(jnp.ones((kc, kc), _F32), k=-1).astype(_CD)
    qspec = lambda w: pl.BlockSpec((1, qb, w), lambda bi, j: (bi, j, 0))
    kspec = pl.BlockSpec((1, s, LANES), lambda bi, j: (bi, 0, 0))
    return pl.pallas_call(
        functools.partial(_dsa_kernel, n_sel=n_sel, qb=qb, kc=kc),
        grid=(b, nq),
        in_specs=[qspec(qa.shape[2]), qspec(qi.shape[2]),
                  pl.BlockSpec((SUBLANES, qb), lambda bi, j: (0, bi * nq + j)),
                  kspec, kspec,
                  pl.BlockSpec((LANES, s), lambda bi, j: (0, bi)),
                  pl.BlockSpec((kc, kc), lambda bi, j: (0, 0))],
        out_specs=pl.BlockSpec((1, qb, A_HEADS * HEAD_DIM), lambda bi, j: (bi, j, 0)),
        out_shape=jax.ShapeDtypeStruct((b, s, A_HEADS * HEAD_DIM), _CD),
        scratch_shapes=[pltpu.VMEM((s, qb), jnp.int32),
                        pltpu.VMEM((LANES, A_HEADS * qb), _F32),
                        pltpu.VMEM((kc // 2, A_HEADS * qb), _F32),
                        pltpu.VMEM((kc // 2, A_HEADS * qb), _F32)],
        compiler_params=_params("parallel", "arbitrary"),
        name="dsa_attention",
    )(qa, qi, wit, sa, sb, vat, tri)


def _diff_kernel(lam_ref, g_ref, q_ref, k_ref, vt_ref, o_ref, acc_ref, sa_stage_ref, sb_stage_ref,
                 *, lam_init, tq):
    j = pl.program_id(1)
    hk = tq // 2
    lv = lam_ref[...]
    lam = (jnp.exp(jnp.sum(lv[0:1] * lv[1:2], axis=1, keepdims=True))
           - jnp.exp(jnp.sum(lv[2:3] * lv[3:4], axis=1, keepdims=True)) + lam_init)
    low = lax.broadcasted_iota(jnp.int32, (tq, LANES), 1) < HEAD_DIM
    zero = jnp.zeros((), _CD)
    krow = lax.broadcasted_iota(jnp.int32, (hk, 2 * tq), 0)
    qcol = lax.broadcasted_iota(jnp.int32, (hk, 2 * tq), 1)
    qidx = jnp.where(qcol >= tq, qcol - tq, qcol)
    g = g_ref[...] * (1.0 - lam_init)

    for h in range(B_HEADS):
        hs = slice(h * LANES, (h + 1) * LANES)
        qs = q_ref[0, :, hs]
        qst = jnp.concatenate([jnp.where(low, qs, zero), jnp.where(low, zero, qs)], axis=0)
        acc_ref[...] = jnp.zeros_like(acc_ref)

        def qk(row0, s_ref, hs=hs, qst=qst):
            s_ref[...] = _dot_nt(k_ref[0, pl.ds(row0, hk), hs], qst)

        def consume(row0, s_ref, carry, diag_row0=None, hs=hs):
            m, l = carry
            s = s_ref[...]
            if diag_row0 is not None:
                s = jnp.where(krow + diag_row0 <= qidx, s, _NEG)
            m_new = jnp.maximum(m, jnp.max(s, axis=0, keepdims=True))
            p = jnp.exp2(s - m_new)
            alpha = jnp.exp2(m - m_new)
            l = alpha * l + jnp.sum(p, axis=0, keepdims=True)
            acc_ref[...] = alpha * acc_ref[...] + _dot(vt_ref[hs, pl.ds(row0, hk)], p.astype(_CD))
            return m_new, l

        qk(0, sa_stage_ref)

        def body(c, carry, qk=qk, consume=consume):
            off = pl.multiple_of(c * tq, tq)
            qk(off + hk, sb_stage_ref)
            carry = consume(off, sa_stage_ref, carry)
            qk(off + tq, sa_stage_ref)
            return consume(off + hk, sb_stage_ref, carry)

        init = (jnp.full((1, 2 * tq), _NEG, _F32), jnp.zeros((1, 2 * tq), _F32))
        carry = lax.fori_loop(0, j, body, init)
        off = pl.multiple_of(j * tq, tq)
        qk(off + hk, sb_stage_ref)
        carry = consume(off, sa_stage_ref, carry, diag_row0=0)
        _, l = consume(off + hk, sb_stage_ref, carry, diag_row0=hk)
        o_t = acc_ref[...] / l
        d = (o_t[:, 0:tq] - lam * o_t[:, tq:2 * tq]).T
        o_ref[0, :, hs] = _rms(d, g).astype(o_ref.dtype)


def _diff_attention(lams, g, q, k, vt, lam_init):
    b, s, w = q.shape
    tq = TQ_DIFF
    full = pl.BlockSpec((1, s, w), lambda bi, j: (bi, 0, 0))
    return pl.pallas_call(
        functools.partial(_diff_kernel, lam_init=lam_init, tq=tq),
        grid=(b, s // tq),
        in_specs=[pl.BlockSpec(lams.shape, lambda bi, j: (0, 0)),
                  pl.BlockSpec(g.shape, lambda bi, j: (0, 0)),
                  pl.BlockSpec((1, tq, w), lambda bi, j: (bi, j, 0)),
                  full,
                  pl.BlockSpec((w, s), lambda bi, j: (0, bi))],
        out_specs=pl.BlockSpec((1, tq, w), lambda bi, j: (bi, j, 0)),
        out_shape=jax.ShapeDtypeStruct((b, s, w), _CD),
        scratch_shapes=[pltpu.VMEM((LANES, 2 * tq), _F32),
                        pltpu.VMEM((tq // 2, 2 * tq), _F32),
                        pltpu.VMEM((tq // 2, 2 * tq), _F32)],
        compiler_params=_params("parallel", "arbitrary"),
        name="diff_attention",
    )(lams, g, q, k, vt)


def _mem_kv_kernel(mem_ref, g_ref, w_ref, k_ref, v_ref):
    d = mem_ref.shape[2]
    mn = _rms(mem_ref[0], g_ref[...]).astype(_CD)
    k_ref[0] = _dot(mn, w_ref[:, 0:d]).astype(k_ref.dtype)
    v_ref[0] = _dot(mn, w_ref[:, d:2 * d]).astype(v_ref.dtype)


def _mem_kv(mem, g, w_kv):
    b, m, d = mem.shape
    blk = pl.BlockSpec((1, m, d), lambda bi: (bi, 0, 0))
    return pl.pallas_call(
        _mem_kv_kernel,
        grid=(b,),
        in_specs=[blk, pl.BlockSpec((1, d), lambda bi: (0, 0)),
                  pl.BlockSpec((d, 2 * d), lambda bi: (0, 0))],
        out_specs=[blk, blk],
        out_shape=[jax.ShapeDtypeStruct((b, m, d), _CD)] * 2,
        compiler_params=_params("parallel"),
        name="mem_kv",
    )(mem, g, w_kv)


def _mix_mem_kernel(h_ref, oa_ref, ob_ref, woa_ref, wob_ref, g_ref, wq_ref, mk_ref, mv_ref,
                    wo_ref, o_ref):
    h1 = h_ref[0] + _dot(oa_ref[0], woa_ref[...]) + _dot(ob_ref[0], wob_ref[...])
    xn = _rms(h1, g_ref[...]).astype(_CD)
    q = _dot(xn, wq_ref[...]).astype(_CD)
    hd = q.shape[1] // MEM_HEADS
    heads = []
    for hh in range(MEM_HEADS):
        hs = slice(hh * hd, (hh + 1) * hd)
        s = _dot_nt(q[:, hs], mk_ref[0, :, hs])
        p = jnp.exp(s - jnp.max(s, axis=1, keepdims=True))
        o = _dot(p.astype(_CD), mv_ref[0, :, hs]) / jnp.sum(p, axis=1, keepdims=True)
        heads.append(o.astype(_CD))
    o_ref[0] = h1 + _dot(jnp.concatenate(heads, axis=1), wo_ref[...])


def _mix_mem(h, oa, ob, woa, wob, g, wq, mk, mv, wo):
    b, s, d = h.shape
    tb = TB_MEM
    tok = lambda w: pl.BlockSpec((1, tb, w), lambda bi, j: (bi, j, 0))
    const = lambda a: pl.BlockSpec(a.shape, lambda bi, j: (0,) * a.ndim)
    memspec = pl.BlockSpec((1,) + mk.shape[1:], lambda bi, j: (bi, 0, 0))
    return pl.pallas_call(
        _mix_mem_kernel,
        grid=(b, s // tb),
        in_specs=[tok(d), tok(oa.shape[2]), tok(ob.shape[2]), const(woa), const(wob), const(g),
                  const(wq), memspec, memspec, const(wo)],
        out_specs=tok(d),
        out_shape=jax.ShapeDtypeStruct((b, s, d), _F32),
        compiler_params=_params("parallel", "parallel"),
        name="mix_mem",
    )(h, oa, ob, woa, wob, g, wq, mk, mv, wo)


def _ffn_kernel(h_ref, halo_ref, g_ref, wg_ref, wu_ref, cw_ref, cb_ref, wd_ref, fg_ref, o_ref, *,
                final_norm):
    j = pl.program_id(1)
    tb = h_ref.shape[1]
    g = g_ref[...]
    h = h_ref[0]
    xn = _rms(h, g).astype(_CD)
    xh = _rms(halo_ref[0], g).astype(_CD)
    gate = _dot(jnp.concatenate([xh, xn], axis=0), wg_ref[...])
    keep = jnp.where(j > 0, 1.0, 0.0)
    gate = jnp.concatenate([gate[0:SUBLANES] * keep, gate[SUBLANES:]], axis=0)
    cw = cw_ref[...]
    conv = cb_ref[...] + cw[CONV_WIDTH - 1:CONV_WIDTH] * gate[SUBLANES:SUBLANES + tb]
    for k in range(CONV_WIDTH - 1):
        sh = CONV_WIDTH - 1 - k
        conv = conv + cw[k:k + 1] * gate[SUBLANES - sh:SUBLANES - sh + tb]
    up = _dot(xn, wu_ref[...])
    act = (conv * jax.nn.sigmoid(conv) * up).astype(_CD)
    out = h + _dot(act, wd_ref[...])
    if final_norm:
        out = _rms(out, fg_ref[...])
    o_ref[0] = out


def _ffn(h, g, wg, wu, cw, cb, wd, fg, final_norm):
    b, s, d = h.shape
    tb = TB_FFN
    const = lambda a: pl.BlockSpec(a.shape, lambda bi, j: (0,) * a.ndim)
    per = tb // SUBLANES
    return pl.pallas_call(
        functools.partial(_ffn_kernel, final_norm=final_norm),
        grid=(b, s // tb),
        in_specs=[pl.BlockSpec((1, tb, d), lambda bi, j: (bi, j, 0)),
                  pl.BlockSpec((1, SUBLANES, d), lambda bi, j: (bi, jnp.maximum(j * per - 1, 0), 0)),
                  const(g), const(wg), const(wu), const(cw), const(cb), const(wd), const(fg)],
        out_specs=pl.BlockSpec((1, tb, d), lambda bi, j: (bi, j, 0)),
        out_shape=jax.ShapeDtypeStruct((b, s, d), _F32),
        compiler_params=_params("parallel", "parallel"),
        name="conv_glu",
    )(h, h, g, wg, wu, cw, cb, wd, fg)


def _prep_w_in(w):
    d = w.shape[0]
    o = 0
    parts = {}
    for name, width in (("qa", A_HEADS * HEAD_DIM), ("ka", HEAD_DIM), ("va", HEAD_DIM),
                        ("qi", IDX_HEADS * HEAD_DIM), ("ki", HEAD_DIM), ("wi", IDX_HEADS),
                        ("qb", 2 * B_HEADS * HEAD_DIM), ("kb", 2 * B_HEADS * HEAD_DIM),
                        ("vb", B_HEADS * B_V_DIM)):
        parts[name] = w[:, o:o + width]
        o += width
    scale = HEAD_DIM ** -0.5 * math.log2(math.e)
    z = lambda n: jnp.zeros((d, n), w.dtype)
    cols = [parts["qa"] * scale, parts["qi"], parts["qb"] * scale, parts["kb"],
            parts["ka"], parts["ki"], parts["ki"], parts["ka"], parts["vb"],
            parts["va"], z(LANES - HEAD_DIM),
            parts["wi"] * (HEAD_DIM ** -0.5 * IDX_HEADS ** -0.5), z(LANES - IDX_HEADS)]
    return jnp.concatenate(cols, axis=1).astype(_CD)


def kernel(x, mem, positions, mix_norm_g, w_in, lambda_q1, lambda_k1, lambda_q2, lambda_k2,
           diff_norm_g, w_out, mem_attn_norm_g, mem_norm_g, w_q_mem, w_kv_mem, w_o_mem, ffn_norm_g,
           w_gate_up, conv_w, conv_b, w_down, final_norm_g):
    b, s, d = x.shape
    depth = w_in.shape[0]
    d_ff = w_down.shape[1]
    n_sel = min(TOPK_MAX, s // 4)
    wa = A_HEADS * HEAD_DIM
    mem_scale = (d // MEM_HEADS) ** -0.5
    row = lambda v: v.reshape(1, -1).astype(_F32)

    rope = _rope_tables(positions)
    h = x
    for l in range(depth):
        qa, qi, qb, kb, sa, sb, vbt, vat, wit = _in_proj(
            h.reshape(b * s, d), row(mix_norm_g[l]), _prep_w_in(w_in[l]), rope)
        r3 = lambda t: t.reshape(b, s, t.shape[-1])
        out_a = _dsa_attention(r3(qa), r3(qi), wit, r3(sa), r3(sb), vat, n_sel)
        lam_init = 0.8 - 0.6 * math.exp(-0.3 * l)
        lams = jnp.stack([lambda_q1[l], lambda_k1[l], lambda_q2[l], lambda_k2[l]]).astype(_F32)
        out_b = _diff_attention(lams, row(diff_norm_g[l]), r3(qb), r3(kb), vbt, lam_init)
        mk, mv = _mem_kv(mem, row(mem_norm_g[l]), w_kv_mem[l].astype(_CD))
        h = _mix_mem(h, out_a, out_b, w_out[l, :wa].astype(_CD), w_out[l, wa:].astype(_CD),
                     row(mem_attn_norm_g[l]), (w_q_mem[l] * mem_scale).astype(_CD), mk, mv,
                     w_o_mem[l].astype(_CD))
        h = _ffn(h, row(ffn_norm_g[l]), w_gate_up[l, :, :d_ff].astype(_CD),
                 w_gate_up[l, :, d_ff:].astype(_CD), conv_w[l].astype(_F32), row(conv_b[l]),
                 w_down[l].astype(_CD), row(final_norm_g), l == depth - 1)
    return h
```

```python
import functools
import math

import jax
import jax.numpy as jnp
from jax import lax
from jax.experimental import pallas as pl
from jax.experimental.pallas import tpu as pltpu

A_HEADS = 8
HEAD_DIM = 64
IDX_HEADS = 4
TOPK_MAX = 256
B_HEADS = 4
B_V_DIM = 128
MEM_HEADS = 4
CONV_WIDTH = 3
ROPE_THETA = 500000.0
ROT_DIM = 16
EPS = 1e-6

LANES = 128
SUBLANES = 8
VMEM_LIMIT = 56 * 1024 * 1024

_CD = jnp.bfloat16
_F32 = jnp.float32
_INT_MIN = -(2 ** 31)
_NEG = -1e30

C_QA, C_QI, C_QB, C_KB, C_SA, C_SB, C_VB, C_VA, C_WI, C_END = (
    0, 512, 768, 1280, 1792, 1920, 2048, 2560, 2688, 2816)

TB_PROJ = 512
QB_DSA = 128
KC_DSA = 512
TQ_DIFF = 512
TB_MEM = 512
TB_FFN = 256
COUNT_ROWS = 64


def _dot(a, b):
    return jnp.dot(a, b, preferred_element_type=_F32)


def _dot_nt(a, b):
    return lax.dot_general(a, b, (((1,), (1,)), ((), ())), preferred_element_type=_F32)


def _rms(x, g):
    return x * lax.rsqrt(jnp.mean(x * x, axis=-1, keepdims=True) + EPS) * g


def _params(*sem):
    return pltpu.CompilerParams(dimension_semantics=sem, vmem_limit_bytes=VMEM_LIMIT)


def _rope_kernel(pos_ref, freq_ref, o_ref):
    ang = pos_ref[...] * freq_ref[...]
    lane = lax.broadcasted_iota(jnp.int32, ang.shape, 1) & (HEAD_DIM - 1)
    c, s = jnp.cos(ang), jnp.sin(ang)
    half = ROT_DIM // 2
    o_ref[:, 0:LANES] = jnp.where(lane < ROT_DIM, c, 1.0)
    o_ref[:, LANES:2 * LANES] = jnp.where(lane < half, -s, 0.0)
    o_ref[:, 2 * LANES:3 * LANES] = jnp.where((lane >= half) & (lane < ROT_DIM), s, 0.0)


def _rope_tables(positions):
    n = positions.size
    tb = 1024
    half = ROT_DIM // 2
    inv_freq = ROPE_THETA ** (-jnp.arange(0, ROT_DIM, 2, dtype=_F32) / ROT_DIM)
    lane = jnp.arange(LANES) % HEAD_DIM
    freq = jnp.where(lane < ROT_DIM, inv_freq[lane % half], 0.0).astype(_F32)[None, :]
    pos = positions.astype(_F32).reshape(n, 1)
    return pl.pallas_call(
        _rope_kernel,
        grid=(n // tb,),
        in_specs=[pl.BlockSpec((tb, 1), lambda i: (i, 0)),
                  pl.BlockSpec((1, LANES), lambda i: (0, 0))],
        out_specs=pl.BlockSpec((tb, 3 * LANES), lambda i: (i, 0)),
        out_shape=jax.ShapeDtypeStruct((n, 3 * LANES), _F32),
        compiler_params=_params("parallel"),
        name="rope_tables",
    )(pos, freq)


def _in_proj_kernel(h_ref, g_ref, w_ref, rope_ref, qa_ref, qi_ref, qb_ref, kb_ref, sa_ref,
                    sb_ref, vbt_ref, vat_ref, wit_ref):
    xn = _rms(h_ref[...], g_ref[...]).astype(_CD)
    a = rope_ref[:, 0:LANES]
    bm = rope_ref[:, LANES:2 * LANES]
    bp = rope_ref[:, 2 * LANES:3 * LANES]

    def project(c0, c1, out_ref):
        p = _dot(xn, w_ref[:, c0:c1])
        for i in range((c1 - c0) // LANES):
            xb = p[:, i * LANES:(i + 1) * LANES]
            xb = (xb * a + pltpu.roll(xb, LANES - ROT_DIM // 2, 1) * bm
                  + pltpu.roll(xb, ROT_DIM // 2, 1) * bp)
            out_ref[:, i * LANES:(i + 1) * LANES] = xb.astype(out_ref.dtype)

    project(C_QA, C_QI, qa_ref)
    project(C_QI, C_QB, qi_ref)
    project(C_QB, C_KB, qb_ref)
    project(C_KB, C_SA, kb_ref)
    project(C_SA, C_SB, sa_ref)
    project(C_SB, C_VB, sb_ref)
    vbt_ref[...] = _dot(xn, w_ref[:, C_VB:C_VA]).T.astype(vbt_ref.dtype)
    pv = _dot(xn, w_ref[:, C_VA:C_WI])
    lane = lax.broadcasted_iota(jnp.int32, pv.shape, 1)
    vat_ref[...] = jnp.where(lane == HEAD_DIM, 1.0, pv).T.astype(vat_ref.dtype)
    wit_ref[...] = _dot(xn, w_ref[:, C_WI:C_END]).T[0:SUBLANES]


def _in_proj(h2d, g, w, rope):
    n, d = h2d.shape
    tb = TB_PROJ
    widths = (C_QI - C_QA, C_QB - C_QI, C_KB - C_QB, C_SA - C_KB, C_SB - C_SA, C_VB - C_SB)
    out_shape = [jax.ShapeDtypeStruct((n, wd), _CD) for wd in widths]
    out_specs = [pl.BlockSpec((tb, s.shape[1]), lambda i: (i, 0)) for s in out_shape]
    for rows, dt in ((C_VA - C_VB, _CD), (C_WI - C_VA, _CD), (SUBLANES, _F32)):
        out_shape.append(jax.ShapeDtypeStruct((rows, n), dt))
        out_specs.append(pl.BlockSpec((rows, tb), lambda i: (0, i)))
    return pl.pallas_call(
        _in_proj_kernel,
        grid=(n // tb,),
        in_specs=[pl.BlockSpec((tb, d), lambda i: (i, 0)),
                  pl.BlockSpec((1, d), lambda i: (0, 0)),
                  pl.BlockSpec((d, C_END), lambda i: (0, 0)),
                  pl.BlockSpec((tb, 3 * LANES), lambda i: (i, 0))],
        out_specs=out_specs,
        out_shape=out_shape,
        compiler_params=_params("parallel"),
        name="in_proj",
    )(h2d, g, w, rope)


def _dsa_kernel(qa_ref, qi_ref, wit_ref, sa_ref, sb_ref, vat_ref, tri_ref, o_ref,
                keys_ref, half_ref, acc_ref, sa_stage_ref, sb_stage_ref, *, n_sel, qb, kc):
    j = pl.program_id(1)
    q0 = j * qb
    nc = (q0 + qb + kc - 1) // kc
    low = lax.broadcasted_iota(jnp.int32, (qb, LANES), 1) < HEAD_DIM
    zero = jnp.zeros((), _CD)

    def lo(x):
        return jnp.where(low, x, zero)

    def hi(x):
        return jnp.where(low, zero, x)

    qi = qi_ref[0]
    qi_even = jnp.concatenate([lo(qi[:, 0:LANES]), lo(qi[:, LANES:2 * LANES])], axis=0)
    qi_odd = jnp.concatenate([hi(qi[:, 0:LANES]), hi(qi[:, LANES:2 * LANES])], axis=0)
    wi = wit_ref[...]
    hk = kc // 2
    s_total = keys_ref.shape[0]
    row_minus_col = (lax.broadcasted_iota(jnp.int32, (kc, qb), 0)
                     - lax.broadcasted_iota(jnp.int32, (kc, qb), 1))

    def score_body(c, carry):
        off = pl.multiple_of(c * kc, kc)
        sa = sa_ref[0, pl.ds(off, kc), :]
        sb = sb_ref[0, pl.ds(off, kc), :]
        de = jnp.maximum(_dot_nt(sb, qi_even), 0.0)
        do = jnp.maximum(_dot_nt(sa, qi_odd), 0.0)
        sc = (de[:, 0:qb] * wi[0:1] + do[:, 0:qb] * wi[1:2]
              + de[:, qb:2 * qb] * wi[2:3] + do[:, qb:2 * qb] * wi[3:4])
        bits = lax.bitcast_convert_type(sc, jnp.int32)
        key = bits ^ ((bits >> 31) & 0x7FFFFFFF)
        key = jnp.where(row_minus_col <= q0 - off, key, _INT_MIN)
        keys_ref[pl.ds(off, kc), :] = key
        half_ref[pl.ds(off, kc), :] = (key >> 16).astype(jnp.int16)
        return carry

    lax.fori_loop(0, nc, score_body, 0)

    def next_half(off):
        return pl.multiple_of(jnp.minimum(off + kc, s_total - hk), hk)

    def count_ge(cand):
        def body(c, acc):
            off = pl.multiple_of(c * kc, kc)
            ge = jnp.where(keys_ref[pl.ds(off, kc), :] >= cand, 1.0, 0.0)
            return acc + jnp.sum(ge.reshape(kc // COUNT_ROWS, COUNT_ROWS, qb), axis=0)
        acc = lax.fori_loop(0, nc, body, jnp.zeros((COUNT_ROWS, qb), _F32))
        return jnp.sum(acc, axis=0, keepdims=True)

    def count_ge16(cand):
        cand16 = cand.astype(jnp.int16)

        def body(c, acc):
            off = pl.multiple_of(c * kc, kc)
            ge = jnp.where(half_ref[pl.ds(off, kc), :] >= cand16, jnp.int16(1), jnp.int16(0))
            parts = [ge[r * COUNT_ROWS:(r + 1) * COUNT_ROWS] for r in range(kc // COUNT_ROWS)]
            while len(parts) > 1:
                parts = [parts[i] + parts[i + 1] for i in range(0, len(parts), 2)]
            return acc + parts[0]
        acc = lax.fori_loop(0, nc, body, jnp.zeros((COUNT_ROWS, qb), jnp.int16))
        return jnp.sum(acc.astype(_F32), axis=0, keepdims=True)

    def search16(base, init):
        def body(it, carry):
            cur, n_cur = carry
            cand = cur + jnp.left_shift(jnp.int32(1), 15 - it)
            n = base + count_ge16(cand)
            ok = n >= n_sel
            return jnp.where(ok, cand, cur), jnp.where(ok, n, n_cur)
        return lax.fori_loop(0, 16, body, init)

    i16_min = -(2 ** 15)
    t_hi, n_hi = search16(0.0, (jnp.full((1, qb), i16_min, jnp.int32),
                                jnp.full((1, qb), float(n_sel), _F32)))
    above = count_ge16(t_hi + 1)

    def low_half_body(c, carry):
        off = pl.multiple_of(c * kc, kc)
        key = keys_ref[pl.ds(off, kc), :]
        low16 = (key & 0xFFFF) + i16_min
        half_ref[pl.ds(off, kc), :] = jnp.where((key >> 16) == t_hi, low16, i16_min).astype(jnp.int16)
        return carry

    lax.fori_loop(0, nc, low_half_body, 0)
    t_lo, n_ge = search16(above, (jnp.full((1, qb), i16_min, jnp.int32), n_hi))
    thr = jnp.left_shift(t_hi, 16) | ((t_lo - i16_min) & 0xFFFF)
    floor_row = thr == _INT_MIN
    need = jnp.where(floor_row, 0.0, n_sel - count_ge(thr + 1))
    has_ties = jnp.max(jnp.where((n_ge > n_sel) & jnp.logical_not(floor_row), 1.0, 0.0)) > 0.0
    thr_all_equal = jnp.where(floor_row, _INT_MIN + 1, thr)

    qa = qa_ref[0]
    npair = A_HEADS // 2
    qa_even = jnp.concatenate([lo(qa[:, p * LANES:(p + 1) * LANES]) for p in range(npair)], axis=0)
    qa_odd = jnp.concatenate([hi(qa[:, p * LANES:(p + 1) * LANES]) for p in range(npair)], axis=0)
    acc_ref[...] = jnp.zeros_like(acc_ref)

    @pl.when(has_ties)
    def _():
        def retire_body(c, eq_seen):
            off = pl.multiple_of(c * kc, kc)
            key = keys_ref[pl.ds(off, kc), :]
            eq = key == thr
            eqf = jnp.where(eq, 1.0, 0.0)
            rank = _dot(tri_ref[...], eqf.astype(_CD)) + eq_seen
            keys_ref[pl.ds(off, kc), :] = jnp.where(eq & (rank >= need), _INT_MIN, key)
            return eq_seen + jnp.sum(eqf, axis=0, keepdims=True)
        lax.fori_loop(0, nc, retire_body, jnp.zeros((1, qb), _F32))

    def qk(row0, s_ref):
        s_even = _dot_nt(sa_ref[0, pl.ds(row0, hk), :], qa_even)
        s_odd = _dot_nt(sb_ref[0, pl.ds(row0, hk), :], qa_odd)
        for p in range(npair):
            s_ref[p] = s_even[:, p * qb:(p + 1) * qb]
            s_ref[npair + p] = s_odd[:, p * qb:(p + 1) * qb]

    def consume(row0, s_ref, m):
        bias = jnp.where(keys_ref[pl.ds(row0, hk), :] >= thr_all_equal, 0.0, _NEG)
        ps, m_news = [], []
        for blk in range(A_HEADS):
            cs = slice(blk * qb, (blk + 1) * qb)
            s = s_ref[blk] + bias
            m_new = jnp.maximum(m[:, cs], jnp.max(s, axis=0, keepdims=True))
            ps.append(jnp.exp2(s - m_new).astype(_CD))
            m_news.append(m_new)
        m_new = jnp.concatenate(m_news, axis=1)
        pv = _dot(vat_ref[:, pl.ds(row0, hk)], jnp.concatenate(ps, axis=1))
        acc_ref[...] = jnp.exp2(m - m_new) * acc_ref[...] + pv
        return m_new

    qk(0, sa_stage_ref)

    def attn_body(c, m):
        off = pl.multiple_of(c * kc, kc)
        qk(off + hk, sb_stage_ref)
        m = consume(off, sa_stage_ref, m)
        qk(next_half(off), sa_stage_ref)
        return consume(off + hk, sb_stage_ref, m)

    lax.fori_loop(0, nc, attn_body, jnp.full((1, A_HEADS * qb), _NEG, _F32))

    acc = acc_ref[...]
    out_t = acc / acc[HEAD_DIM:HEAD_DIM + 1, :]
    for p in range(npair):
        even = out_t[:, p * qb:(p + 1) * qb].T
        odd = out_t[:, (npair + p) * qb:(npair + p + 1) * qb].T
        blk = jnp.where(low, even, pltpu.roll(odd, HEAD_DIM, 1))
        o_ref[0, :, p * LANES:(p + 1) * LANES] = blk.astype(o_ref.dtype)


def _dsa_attention(qa, qi, wit, sa, sb, vat, n_sel):
    b, s, _ = qa.shape
    qb, kc = QB_DSA, KC_DSA
    nq = s // qb
    tri = jnp.tril(jnp.ones((kc, kc), _F32), k=-1).astype(_CD)
    qspec = lambda w: pl.BlockSpec((1, qb, w), lambda bi, j: (bi, j, 0))
    kspec = pl.BlockSpec((1, s, LANES), lambda bi, j: (bi, 0, 0))
    return pl.pallas_call(
        functools.partial(_dsa_kernel, n_sel=n_sel, qb=qb, kc=kc),
        grid=(b, nq),
        in_specs=[qspec(qa.shape[2]), qspec(qi.shape[2]),
                  pl.BlockSpec((SUBLANES, qb), lambda bi, j: (0, bi * nq + j)),
                  kspec, kspec,
                  pl.BlockSpec((LANES, s), lambda bi, j: (0, bi)),
                  pl.BlockSpec((kc, kc), lambda bi, j: (0, 0))],
        out_specs=pl.BlockSpec((1, qb, A_HEADS * HEAD_DIM), lambda bi, j: (bi, j, 0)),
        out_shape=jax.ShapeDtypeStruct((b, s, A_HEADS * HEAD_DIM), _CD),
        scratch_shapes=[pltpu.VMEM((s, qb), jnp.int32),
                        pltpu.VMEM((s, qb), jnp.int16),
                        pltpu.VMEM((LANES, A_HEADS * qb), _F32),
                        pltpu.VMEM((A_HEADS, kc // 2, qb), _F32),
                        pltpu.VMEM((A_HEADS, kc // 2, qb), _F32)],
        compiler_params=_params("parallel", "arbitrary"),
        name="dsa_attention",
    )(qa, qi, wit, sa, sb, vat, tri)


def _diff_kernel(lam_ref, g_ref, q_ref, k_ref, vt_ref, o_ref, acc_ref, sa_stage_ref, sb_stage_ref,
                 *, lam_init, tq):
    j = pl.program_id(1)
    hk = tq // 2
    lv = lam_ref[...]
    lam = (jnp.exp(jnp.sum(lv[0:1] * lv[1:2], axis=1, keepdims=True))
           - jnp.exp(jnp.sum(lv[2:3] * lv[3:4], axis=1, keepdims=True)) + lam_init)
    low = lax.broadcasted_iota(jnp.int32, (tq, LANES), 1) < HEAD_DIM
    zero = jnp.zeros((), _CD)
    krow = lax.broadcasted_iota(jnp.int32, (hk, 2 * tq), 0)
    qcol = lax.broadcasted_iota(jnp.int32, (hk, 2 * tq), 1)
    qidx = jnp.where(qcol >= tq, qcol - tq, qcol)
    g = g_ref[...] * (1.0 - lam_init)

    for h in range(B_HEADS):
        hs = slice(h * LANES, (h + 1) * LANES)
        qs = q_ref[0, :, hs]
        qst = jnp.concatenate([jnp.where(low, qs, zero), jnp.where(low, zero, qs)], axis=0)
        acc_ref[...] = jnp.zeros_like(acc_ref)

        def qk(row0, s_ref, hs=hs, qst=qst):
            s_ref[...] = _dot_nt(k_ref[0, pl.ds(row0, hk), hs], qst)

        def consume(row0, s_ref, carry, diag_row0=None, hs=hs):
            m, l = carry
            s = s_ref[...]
            if diag_row0 is not None:
                s = jnp.where(krow + diag_row0 <= qidx, s, _NEG)
            m_new = jnp.maximum(m, jnp.max(s, axis=0, keepdims=True))
            p = jnp.exp2(s - m_new)
            alpha = jnp.exp2(m - m_new)
            l = alpha * l + jnp.sum(p, axis=0, keepdims=True)
            acc_ref[...] = alpha * acc_ref[...] + _dot(vt_ref[hs, pl.ds(row0, hk)], p.astype(_CD))
            return m_new, l

        qk(0, sa_stage_ref)

        def body(c, carry, qk=qk, consume=consume):
            off = pl.multiple_of(c * tq, tq)
            qk(off + hk, sb_stage_ref)
            carry = consume(off, sa_stage_ref, carry)
            qk(off + tq, sa_stage_ref)
            return consume(off + hk, sb_stage_ref, carry)

        init = (jnp.full((1, 2 * tq), _NEG, _F32), jnp.zeros((1, 2 * tq), _F32))
        carry = lax.fori_loop(0, j, body, init)
        off = pl.multiple_of(j * tq, tq)
        qk(off + hk, sb_stage_ref)
        carry = consume(off, sa_stage_ref, carry, diag_row0=0)
        _, l = consume(off + hk, sb_stage_ref, carry, diag_row0=hk)
        o_t = acc_ref[...] / l
        d = (o_t[:, 0:tq] - lam * o_t[:, tq:2 * tq]).T
        o_ref[0, :, hs] = _rms(d, g).astype(o_ref.dtype)


def _diff_attention(lams, g, q, k, vt, lam_init):
    b, s, w = q.shape
    tq = TQ_DIFF
    full = pl.BlockSpec((1, s, w), lambda bi, j: (bi, 0, 0))
    return pl.pallas_call(
        functools.partial(_diff_kernel, lam_init=lam_init, tq=tq),
        grid=(b, s // tq),
        in_specs=[pl.BlockSpec(lams.shape, lambda bi, j: (0, 0)),
                  pl.BlockSpec(g.shape, lambda bi, j: (0, 0)),
                  pl.BlockSpec((1, tq, w), lambda bi, j: (bi, j, 0)),
                  full,
                  pl.BlockSpec((w, s), lambda bi, j: (0, bi))],
        out_specs=pl.BlockSpec((1, tq, w), lambda bi, j: (bi, j, 0)),
        out_shape=jax.ShapeDtypeStruct((b, s, w), _CD),
        scratch_shapes=[pltpu.VMEM((LANES, 2 * tq), _F32),
                        pltpu.VMEM((tq // 2, 2 * tq), _F32),
                        pltpu.VMEM((tq // 2, 2 * tq), _F32)],
        compiler_params=_params("parallel", "arbitrary"),
        name="diff_attention",
    )(lams, g, q, k, vt)


def _mem_kv_kernel(mem_ref, g_ref, w_ref, k_ref, v_ref):
    d = mem_ref.shape[2]
    mn = _rms(mem_ref[0], g_ref[...]).astype(_CD)
    k_ref[0] = _dot(mn, w_ref[:, 0:d]).astype(k_ref.dtype)
    v_ref[0] = _dot(mn, w_ref[:, d:2 * d]).astype(v_ref.dtype)


def _mem_kv(mem, g, w_kv):
    b, m, d = mem.shape
    blk = pl.BlockSpec((1, m, d), lambda bi: (bi, 0, 0))
    return pl.pallas_call(
        _mem_kv_kernel,
        grid=(b,),
        in_specs=[blk, pl.BlockSpec((1, d), lambda bi: (0, 0)),
                  pl.BlockSpec((d, 2 * d), lambda bi: (0, 0))],
        out_specs=[blk, blk],
        out_shape=[jax.ShapeDtypeStruct((b, m, d), _CD)] * 2,
        compiler_params=_params("parallel"),
        name="mem_kv",
    )(mem, g, w_kv)


def _mix_mem_kernel(h_ref, oa_ref, ob_ref, woa_ref, wob_ref, g_ref, wq_ref, mk_ref, mv_ref,
                    wo_ref, o_ref):
    h1 = h_ref[0] + _dot(oa_ref[0], woa_ref[...]) + _dot(ob_ref[0], wob_ref[...])
    xn = _rms(h1, g_ref[...]).astype(_CD)
    q = _dot(xn, wq_ref[...]).astype(_CD)
    hd = q.shape[1] // MEM_HEADS
    heads = []
    for hh in range(MEM_HEADS):
        hs = slice(hh * hd, (hh + 1) * hd)
        s = _dot_nt(q[:, hs], mk_ref[0, :, hs])
        p = jnp.exp(s - jnp.max(s, axis=1, keepdims=True))
        o = _dot(p.astype(_CD), mv_ref[0, :, hs]) / jnp.sum(p, axis=1, keepdims=True)
        heads.append(o.astype(_CD))
    o_ref[0] = h1 + _dot(jnp.concatenate(heads, axis=1), wo_ref[...])


def _mix_mem(h, oa, ob, woa, wob, g, wq, mk, mv, wo):
    b, s, d = h.shape
    tb = TB_MEM
    tok = lambda w: pl.BlockSpec((1, tb, w), lambda bi, j: (bi, j, 0))
    const = lambda a: pl.BlockSpec(a.shape, lambda bi, j: (0,) * a.ndim)
    memspec = pl.BlockSpec((1,) + mk.shape[1:], lambda bi, j: (bi, 0, 0))
    return pl.pallas_call(
        _mix_mem_kernel,
        grid=(b, s // tb),
        in_specs=[tok(d), tok(oa.shape[2]), tok(ob.shape[2]), const(woa), const(wob), const(g),
                  const(wq), memspec, memspec, const(wo)],
        out_specs=tok(d),
        out_shape=jax.ShapeDtypeStruct((b, s, d), _F32),
        compiler_params=_params("parallel", "parallel"),
        name="mix_mem",
    )(h, oa, ob, woa, wob, g, wq, mk, mv, wo)


def _ffn_kernel(h_ref, halo_ref, g_ref, wg_ref, wu_ref, cw_ref, cb_ref, wd_ref, fg_ref, o_ref, *,
                final_norm):
    j = pl.program_id(1)
    tb = h_ref.shape[1]
    g = g_ref[...]
    h = h_ref[0]
    xn = _rms(h, g).astype(_CD)
    xh = _rms(halo_ref[0], g).astype(_CD)
    gate = _dot(jnp.concatenate([xh, xn], axis=0), wg_ref[...])
    keep = jnp.where(j > 0, 1.0, 0.0)
    gate = jnp.concatenate([gate[0:SUBLANES] * keep, gate[SUBLANES:]], axis=0)
    cw = cw_ref[...]
    conv = cb_ref[...] + cw[CONV_WIDTH - 1:CONV_WIDTH] * gate[SUBLANES:SUBLANES + tb]
    for k in range(CONV_WIDTH - 1):
        sh = CONV_WIDTH - 1 - k
        conv = conv + cw[k:k + 1] * gate[SUBLANES - sh:SUBLANES - sh + tb]
    up = _dot(xn, wu_ref[...])
    act = (conv * jax.nn.sigmoid(conv) * up).astype(_CD)
    out = h + _dot(act, wd_ref[...])
    if final_norm:
        out = _rms(out, fg_ref[...])
    o_ref[0] = out


def _ffn(h, g, wg, wu, cw, cb, wd, fg, final_norm):
    b, s, d = h.shape
    tb = TB_FFN
    const = lambda a: pl.BlockSpec(a.shape, lambda bi, j: (0,) * a.ndim)
    per = tb // SUBLANES
    return pl.pallas_call(
        functools.partial(_ffn_kernel, final_norm=final_norm),
        grid=(b, s // tb),
        in_specs=[pl.BlockSpec((1, tb, d), lambda bi, j: (bi, j, 0)),
                  pl.BlockSpec((1, SUBLANES, d), lambda bi, j: (bi, jnp.maximum(j * per - 1, 0), 0)),
                  const(g), const(wg), const(wu), const(cw), const(cb), const(wd), const(fg)],
        out_specs=pl.BlockSpec((1, tb, d), lambda bi, j: (bi, j, 0)),
        out_shape=jax.ShapeDtypeStruct((b, s, d), _F32),
        compiler_params=_params("parallel", "parallel"),
        name="conv_glu",
    )(h, h, g, wg, wu, cw, cb, wd, fg)


def _prep_w_in(w):
    d = w.shape[0]
    o = 0
    parts = {}
    for name, width in (("qa", A_HEADS * HEAD_DIM), ("ka", HEAD_DIM), ("va", HEAD_DIM),
                        ("qi", IDX_HEADS * HEAD_DIM), ("ki", HEAD_DIM), ("wi", IDX_HEADS),
                        ("qb", 2 * B_HEADS * HEAD_DIM), ("kb", 2 * B_HEADS * HEAD_DIM),
                        ("vb", B_HEADS * B_V_DIM)):
        parts[name] = w[:, o:o + width]
        o += width
    scale = HEAD_DIM ** -0.5 * math.log2(math.e)
    z = lambda n: jnp.zeros((d, n), w.dtype)
    cols = [parts["qa"] * scale, parts["qi"], parts["qb"] * scale, parts["kb"],
            parts["ka"], parts["ki"], parts["ki"], parts["ka"], parts["vb"],
            parts["va"], z(LANES - HEAD_DIM),
            parts["wi"] * (HEAD_DIM ** -0.5 * IDX_HEADS ** -0.5), z(LANES - IDX_HEADS)]
    return jnp.concatenate(cols, axis=1).astype(_CD)


def kernel(x, mem, positions, mix_norm_g, w_in, lambda_q1, lambda_k1, lambda_q2, lambda_k2,
           diff_norm_g, w_out, mem_attn_norm_g, mem_norm_g, w_q_mem, w_kv_mem, w_o_mem, ffn_norm_g,
           w_gate_up, conv_w, conv_b, w_down, final_norm_g):
    b, s, d = x.shape
    depth = w_in.shape[0]
    d_ff = w_down.shape[1]
    n_sel = min(TOPK_MAX, s // 4)
    wa = A_HEADS * HEAD_DIM
    mem_scale = (d // MEM_HEADS) ** -0.5
    row = lambda v: v.reshape(1, -1).astype(_F32)

    rope = _rope_tables(positions)
    h = x
    for l in range(depth):
        qa, qi, qb, kb, sa, sb, vbt, vat, wit = _in_proj(
            h.reshape(b * s, d), row(mix_norm_g[l]), _prep_w_in(w_in[l]), rope)
        r3 = lambda t: t.reshape(b, s, t.shape[-1])
        out_a = _dsa_attention(r3(qa), r3(qi), wit, r3(sa), r3(sb), vat, n_sel)
        lam_init = 0.8 - 0.6 * math.exp(-0.3 * l)
        lams = jnp.stack([lambda_q1[l], lambda_k1[l], lambda_q2[l], lambda_k2[l]]).astype(_F32)
        out_b = _diff_attention(lams, row(diff_norm_g[l]), r3(qb), r3(kb), vbt, lam_init)
        mk, mv = _mem_kv(mem, row(mem_norm_g[l]), w_kv_mem[l].astype(_CD))
        h = _mix_mem(h, out_a, out_b, w_out[l, :wa].astype(_CD), w_out[l, wa:].astype(_CD),
                     row(mem_attn_norm_g[l]), (w_q_mem[l] * mem_scale).astype(_CD), mk, mv,
                     w_o_mem[l].astype(_CD))
        h = _ffn(h, row(ffn_norm_g[l]), w_gate_up[l, :, :d_ff].astype(_CD),
                 w_gate_up[l, :, d_ff:].astype(_CD), conv_w[l].astype(_F32), row(conv_b[l]),
                 w_down[l].astype(_CD), row(final_norm_g), l == depth - 1)
    return h
```

```python
import functools
import math

import jax
import jax.numpy as jnp
from jax import lax
from jax.experimental import pallas as pl
from jax.experimental.pallas import tpu as pltpu

A_HEADS = 8
HEAD_DIM = 64
IDX_HEADS = 4
TOPK_MAX = 256
B_HEADS = 4
B_V_DIM = 128
MEM_HEADS = 4
CONV_WIDTH = 3
ROPE_THETA = 500000.0
ROT_DIM = 16
EPS = 1e-6

LANES = 128
SUBLANES = 8
VMEM_LIMIT = 56 * 1024 * 1024

_CD = jnp.bfloat16
_F32 = jnp.float32
_INT_MIN = -(2 ** 31)
_NEG = -1e30

C_QA, C_QI, C_QB, C_KB, C_KK, C_VB, C_VA, C_WI, C_END = (
    0, 512, 768, 1280, 1792, 1920, 2432, 2560, 2688)

TB_PROJ = 512
QB_DSA = 512
KC_DSA = 512
TQ_DIFF = 512
TB_MEM = 512
TB_FFN = 256
COUNT_ROWS = 16


def _dot(a, b):
    return jnp.dot(a, b, preferred_element_type=_F32)


def _dot_nt(a, b):
    return lax.dot_general(a, b, (((1,), (1,)), ((), ())), preferred_element_type=_F32)


def _rms(x, g):
    return x * lax.rsqrt(jnp.mean(x * x, axis=-1, keepdims=True) + EPS) * g


def _params(*sem):
    return pltpu.CompilerParams(dimension_semantics=sem, vmem_limit_bytes=VMEM_LIMIT)


def _rope_kernel(pos_ref, freq_ref, o_ref):
    ang = pos_ref[...] * freq_ref[...]
    lane = lax.broadcasted_iota(jnp.int32, ang.shape, 1) & (HEAD_DIM - 1)
    c, s = jnp.cos(ang), jnp.sin(ang)
    half = ROT_DIM // 2
    o_ref[:, 0:LANES] = jnp.where(lane < ROT_DIM, c, 1.0)
    o_ref[:, LANES:2 * LANES] = jnp.where(lane < half, -s, 0.0)
    o_ref[:, 2 * LANES:3 * LANES] = jnp.where((lane >= half) & (lane < ROT_DIM), s, 0.0)


def _rope_tables(positions):
    n = positions.size
    tb = 1024
    half = ROT_DIM // 2
    inv_freq = ROPE_THETA ** (-jnp.arange(0, ROT_DIM, 2, dtype=_F32) / ROT_DIM)
    lane = jnp.arange(LANES) % HEAD_DIM
    freq = jnp.where(lane < ROT_DIM, inv_freq[lane % half], 0.0).astype(_F32)[None, :]
    pos = positions.astype(_F32).reshape(n, 1)
    return pl.pallas_call(
        _rope_kernel,
        grid=(n // tb,),
        in_specs=[pl.BlockSpec((tb, 1), lambda i: (i, 0)),
                  pl.BlockSpec((1, LANES), lambda i: (0, 0))],
        out_specs=pl.BlockSpec((tb, 3 * LANES), lambda i: (i, 0)),
        out_shape=jax.ShapeDtypeStruct((n, 3 * LANES), _F32),
        compiler_params=_params("parallel"),
        name="rope_tables",
    )(pos, freq)


def _in_proj_kernel(h_ref, g_ref, w_ref, rope_ref, qat_ref, qit_ref, qbt_ref, kb_ref, kk_ref,
                    vbt_ref, vat_ref, wit_ref):
    xn = _rms(h_ref[...], g_ref[...]).astype(_CD)
    a = rope_ref[:, 0:LANES]
    bm = rope_ref[:, LANES:2 * LANES]
    bp = rope_ref[:, 2 * LANES:3 * LANES]

    def project(c0, c1, out_ref, transposed):
        p = _dot(xn, w_ref[:, c0:c1])
        for i in range((c1 - c0) // LANES):
            xb = p[:, i * LANES:(i + 1) * LANES]
            xb = (xb * a + pltpu.roll(xb, LANES - ROT_DIM // 2, 1) * bm
                  + pltpu.roll(xb, ROT_DIM // 2, 1) * bp)
            if transposed:
                out_ref[i * LANES:(i + 1) * LANES, :] = xb.T.astype(out_ref.dtype)
            else:
                out_ref[:, i * LANES:(i + 1) * LANES] = xb.astype(out_ref.dtype)

    project(C_QA, C_QI, qat_ref, True)
    project(C_QI, C_QB, qit_ref, True)
    project(C_QB, C_KB, qbt_ref, True)
    project(C_KB, C_KK, kb_ref, False)
    project(C_KK, C_VB, kk_ref, False)
    vbt_ref[...] = _dot(xn, w_ref[:, C_VB:C_VA]).T.astype(vbt_ref.dtype)
    pv = _dot(xn, w_ref[:, C_VA:C_WI])
    lane = lax.broadcasted_iota(jnp.int32, pv.shape, 1)
    vat_ref[...] = jnp.where(lane == HEAD_DIM, 1.0, pv).T.astype(vat_ref.dtype)
    wit_ref[...] = _dot(xn, w_ref[:, C_WI:C_END]).T[0:SUBLANES]


def _in_proj(h2d, g, w, rope):
    n, d = h2d.shape
    tb = TB_PROJ
    tok = lambda wd: (jax.ShapeDtypeStruct((n, wd), _CD), pl.BlockSpec((tb, wd), lambda i: (i, 0)))
    feat = lambda rows, dt=_CD: (jax.ShapeDtypeStruct((rows, n), dt),
                                 pl.BlockSpec((rows, tb), lambda i: (0, i)))
    outs = [feat(C_QI - C_QA), feat(C_QB - C_QI), feat(C_KB - C_QB), tok(C_KK - C_KB),
            tok(C_VB - C_KK), feat(C_VA - C_VB), feat(C_WI - C_VA), feat(SUBLANES, _F32)]
    return pl.pallas_call(
        _in_proj_kernel,
        grid=(n // tb,),
        in_specs=[pl.BlockSpec((tb, d), lambda i: (i, 0)),
                  pl.BlockSpec((1, d), lambda i: (0, 0)),
                  pl.BlockSpec((d, C_END), lambda i: (0, 0)),
                  pl.BlockSpec((tb, 3 * LANES), lambda i: (i, 0))],
        out_specs=[o[1] for o in outs],
        out_shape=[o[0] for o in outs],
        compiler_params=_params("parallel"),
        name="in_proj",
    )(h2d, g, w, rope)


def _dsa_kernel(qat_ref, qit_ref, wit_ref, kk_ref, vat_ref, tri_ref, o_ref,
                keys_ref, acc_ref, sa_stage_ref, sb_stage_ref, *, n_sel, qb, kc):
    j = pl.program_id(1)
    q0 = j * qb
    nc = (q0 + qb + kc - 1) // kc
    hk = kc // 2
    s_total = keys_ref.shape[0]
    zpad = jnp.zeros((HEAD_DIM, qb), _CD)

    qat = qat_ref[...]
    qa_all = jnp.concatenate(
        [jnp.concatenate([qat[h * HEAD_DIM:(h + 1) * HEAD_DIM], zpad], axis=0)
         for h in range(A_HEADS)], axis=1)
    qit = qit_ref[...]
    qi_all = jnp.concatenate(
        [jnp.concatenate([zpad, qit[h * HEAD_DIM:(h + 1) * HEAD_DIM]], axis=0)
         for h in range(IDX_HEADS)], axis=1)

    wi = wit_ref[...]
    row_minus_col = (lax.broadcasted_iota(jnp.int32, (kc, qb), 0)
                     - lax.broadcasted_iota(jnp.int32, (kc, qb), 1))

    def score_body(c, carry):
        off = pl.multiple_of(c * kc, kc)
        d = jnp.maximum(_dot(kk_ref[0, pl.ds(off, kc), :], qi_all), 0.0)
        sc = d[:, 0:qb] * wi[0:1]
        for h in range(1, IDX_HEADS):
            sc = sc + d[:, h * qb:(h + 1) * qb] * wi[h:h + 1]
        bits = lax.bitcast_convert_type(sc, jnp.int32)
        key = bits ^ ((bits >> 31) & 0x7FFFFFFF)
        key = jnp.where(row_minus_col <= q0 - off, key, _INT_MIN)
        keys_ref[pl.ds(off, kc), :] = key
        return carry

    lax.fori_loop(0, nc, score_body, 0)

    def count_ge(cand):
        def body(c, acc):
            off = pl.multiple_of(c * kc, kc)
            ge = jnp.where(keys_ref[pl.ds(off, kc), :] >= cand, 1.0, 0.0)
            return acc + jnp.sum(ge.reshape(kc // COUNT_ROWS, COUNT_ROWS, qb), axis=0)
        acc = lax.fori_loop(0, nc, body, jnp.zeros((COUNT_ROWS, qb), _F32))
        return jnp.sum(acc, axis=0, keepdims=True)

    def search_body(it, carry):
        cur, n_cur = carry
        cand = cur + jnp.left_shift(jnp.int32(1), 31 - it)
        n = count_ge(cand)
        ok = n >= n_sel
        return jnp.where(ok, cand, cur), jnp.where(ok, n, n_cur)

    thr, n_ge = lax.fori_loop(
        0, 32, search_body,
        (jnp.full((1, qb), _INT_MIN, jnp.int32), jnp.full((1, qb), float(n_sel), _F32)))
    floor_row = thr == _INT_MIN
    has_ties = jnp.max(jnp.where((n_ge > n_sel) & jnp.logical_not(floor_row), 1.0, 0.0)) > 0.0
    thr_all_equal = jnp.where(floor_row, _INT_MIN + 1, thr)

    acc_ref[...] = jnp.zeros_like(acc_ref)

    @pl.when(has_ties)
    def _():
        need = jnp.where(floor_row, 0.0, n_sel - count_ge(thr + 1))

        def retire_body(c, eq_seen):
            off = pl.multiple_of(c * kc, kc)
            key = keys_ref[pl.ds(off, kc), :]
            eq = key == thr
            eqf = jnp.where(eq, 1.0, 0.0)
            rank = _dot(tri_ref[...], eqf.astype(_CD)) + eq_seen
            keys_ref[pl.ds(off, kc), :] = jnp.where(eq & (rank >= need), _INT_MIN, key)
            return eq_seen + jnp.sum(eqf, axis=0, keepdims=True)
        lax.fori_loop(0, nc, retire_body, jnp.zeros((1, qb), _F32))

    def qk(row0, s_ref):
        s_ref[...] = _dot(kk_ref[0, pl.ds(row0, hk), :], qa_all)

    def consume(row0, s_ref, m):
        bias = jnp.where(keys_ref[pl.ds(row0, hk), :] >= thr_all_equal, 0.0, _NEG)
        ps, m_news = [], []
        for h in range(A_HEADS):
            cs = slice(h * qb, (h + 1) * qb)
            s = s_ref[:, cs] + bias
            m_new = jnp.maximum(m[:, cs], jnp.max(s, axis=0, keepdims=True))
            ps.append(jnp.exp2(s - m_new).astype(_CD))
            m_news.append(m_new)
        m_new = jnp.concatenate(m_news, axis=1)
        pv = _dot(vat_ref[:, pl.ds(row0, hk)], jnp.concatenate(ps, axis=1))
        acc_ref[...] = jnp.exp2(m - m_new) * acc_ref[...] + pv
        return m_new

    qk(0, sa_stage_ref)

    def attn_body(c, m):
        off = pl.multiple_of(c * kc, kc)
        qk(off + hk, sb_stage_ref)
        m = consume(off, sa_stage_ref, m)
        qk(pl.multiple_of(jnp.minimum(off + kc, s_total - hk), hk), sa_stage_ref)
        return consume(off + hk, sb_stage_ref, m)

    lax.fori_loop(0, nc, attn_body, jnp.full((1, A_HEADS * qb), _NEG, _F32))

    acc = acc_ref[...]
    out_t = acc / acc[HEAD_DIM:HEAD_DIM + 1, :]
    low = lax.broadcasted_iota(jnp.int32, (qb, LANES), 1) < HEAD_DIM
    for p in range(A_HEADS // 2):
        even = out_t[:, (2 * p) * qb:(2 * p + 1) * qb].T
        odd = out_t[:, (2 * p + 1) * qb:(2 * p + 2) * qb].T
        blk = jnp.where(low, even, pltpu.roll(odd, HEAD_DIM, 1))
        o_ref[0, :, p * LANES:(p + 1) * LANES] = blk.astype(o_ref.dtype)


def _dsa_attention(qat, qit, wit, kk, vat, n_sel):
    b, s, _ = kk.shape
    qb, kc = QB_DSA, KC_DSA
    nq = s // qb
    tri = jnp.tril(jnp.ones((kc, kc), _F32), k=-1).astype(_CD)
    qspec = lambda rows: pl.BlockSpec((rows, qb), lambda bi, j: (0, bi * nq + j))
    return pl.pallas_call(
        functools.partial(_dsa_kernel, n_sel=n_sel, qb=qb, kc=kc),
        grid=(b, nq),
        in_specs=[qspec(qat.shape[0]), qspec(qit.shape[0]), qspec(SUBLANES),
                  pl.BlockSpec((1, s, LANES), lambda bi, j: (bi, 0, 0)),
                  pl.BlockSpec((LANES, s), lambda bi, j: (0, bi)),
                  pl.BlockSpec((kc, kc), lambda bi, j: (0, 0))],
        out_specs=pl.BlockSpec((1, qb, A_HEADS * HEAD_DIM), lambda bi, j: (bi, j, 0)),
        out_shape=jax.ShapeDtypeStruct((b, s, A_HEADS * HEAD_DIM), _CD),
        scratch_shapes=[pltpu.VMEM((s, qb), jnp.int32),
                        pltpu.VMEM((LANES, A_HEADS * qb), _F32),
                        pltpu.VMEM((kc // 2, A_HEADS * qb), _F32),
                        pltpu.VMEM((kc // 2, A_HEADS * qb), _F32)],
        compiler_params=_params("parallel", "arbitrary"),
        name="dsa_attention",
    )(qat, qit, wit, kk, vat, tri)


def _diff_kernel(lam_ref, g_ref, qt_ref, k_ref, vt_ref, o_ref, acc_ref, sa_stage_ref, sb_stage_ref,
                 *, lam_init, tq):
    j = pl.program_id(1)
    hk = tq // 2
    lv = lam_ref[...]
    lam = (jnp.exp(jnp.sum(lv[0:1] * lv[1:2], axis=1, keepdims=True))
           - jnp.exp(jnp.sum(lv[2:3] * lv[3:4], axis=1, keepdims=True)) + lam_init)
    zpad = jnp.zeros((HEAD_DIM, tq), _CD)
    krow = lax.broadcasted_iota(jnp.int32, (hk, 2 * tq), 0)
    qcol = lax.broadcasted_iota(jnp.int32, (hk, 2 * tq), 1)
    qidx = jnp.where(qcol >= tq, qcol - tq, qcol)
    g = g_ref[...] * (1.0 - lam_init)

    for h in range(B_HEADS):
        hs = slice(h * LANES, (h + 1) * LANES)
        qt = qt_ref[hs, :]
        q_all = jnp.concatenate([jnp.concatenate([qt[0:HEAD_DIM], zpad], axis=0),
                                 jnp.concatenate([zpad, qt[HEAD_DIM:]], axis=0)], axis=1)
        acc_ref[...] = jnp.zeros_like(acc_ref)

        def qk(row0, s_ref, hs=hs, q_all=q_all):
            s_ref[...] = _dot(k_ref[0, pl.ds(row0, hk), hs], q_all)

        def consume(row0, s_ref, carry, diag_row0=None, hs=hs):
            m, l = carry
            s = s_ref[...]
            if diag_row0 is not None:
                s = jnp.where(krow + diag_row0 <= qidx, s, _NEG)
            m_new = jnp.maximum(m, jnp.max(s, axis=0, keepdims=True))
            p = jnp.exp2(s - m_new)
            alpha = jnp.exp2(m - m_new)
            l = alpha * l + jnp.sum(p, axis=0, keepdims=True)
            acc_ref[...] = alpha * acc_ref[...] + _dot(vt_ref[hs, pl.ds(row0, hk)], p.astype(_CD))
            return m_new, l

        qk(0, sa_stage_ref)

        def body(c, carry, qk=qk, consume=consume):
            off = pl.multiple_of(c * tq, tq)
            qk(off + hk, sb_stage_ref)
            carry = consume(off, sa_stage_ref, carry)
            qk(off + tq, sa_stage_ref)
            return consume(off + hk, sb_stage_ref, carry)

        init = (jnp.full((1, 2 * tq), _NEG, _F32), jnp.zeros((1, 2 * tq), _F32))
        carry = lax.fori_loop(0, j, body, init)
        off = pl.multiple_of(j * tq, tq)
        qk(off + hk, sb_stage_ref)
        carry = consume(off, sa_stage_ref, carry, diag_row0=0)
        _, l = consume(off + hk, sb_stage_ref, carry, diag_row0=hk)
        o_t = acc_ref[...] / l
        d = (o_t[:, 0:tq] - lam * o_t[:, tq:2 * tq]).T
        o_ref[0, :, hs] = _rms(d, g).astype(o_ref.dtype)


def _diff_attention(lams, g, qt, k, vt, lam_init):
    b, s, w = k.shape
    tq = TQ_DIFF
    nq = s // tq
    return pl.pallas_call(
        functools.partial(_diff_kernel, lam_init=lam_init, tq=tq),
        grid=(b, nq),
        in_specs=[pl.BlockSpec(lams.shape, lambda bi, j: (0, 0)),
                  pl.BlockSpec(g.shape, lambda bi, j: (0, 0)),
                  pl.BlockSpec((w, tq), lambda bi, j: (0, bi * nq + j)),
                  pl.BlockSpec((1, s, w), lambda bi, j: (bi, 0, 0)),
                  pl.BlockSpec((w, s), lambda bi, j: (0, bi))],
        out_specs=pl.BlockSpec((1, tq, w), lambda bi, j: (bi, j, 0)),
        out_shape=jax.ShapeDtypeStruct((b, s, w), _CD),
        scratch_shapes=[pltpu.VMEM((LANES, 2 * tq), _F32),
                        pltpu.VMEM((tq // 2, 2 * tq), _F32),
                        pltpu.VMEM((tq // 2, 2 * tq), _F32)],
        compiler_params=_params("parallel", "arbitrary"),
        name="diff_attention",
    )(lams, g, qt, k, vt)


def _mem_kv_kernel(mem_ref, g_ref, w_ref, k_ref, v_ref):
    d = mem_ref.shape[2]
    mn = _rms(mem_ref[0], g_ref[...]).astype(_CD)
    k_ref[0] = _dot(mn, w_ref[:, 0:d]).astype(k_ref.dtype)
    v_ref[0] = _dot(mn, w_ref[:, d:2 * d]).astype(v_ref.dtype)


def _mem_kv(mem, g, w_kv):
    b, m, d = mem.shape
    blk = pl.BlockSpec((1, m, d), lambda bi: (bi, 0, 0))
    return pl.pallas_call(
        _mem_kv_kernel,
        grid=(b,),
        in_specs=[blk, pl.BlockSpec((1, d), lambda bi: (0, 0)),
                  pl.BlockSpec((d, 2 * d), lambda bi: (0, 0))],
        out_specs=[blk, blk],
        out_shape=[jax.ShapeDtypeStruct((b, m, d), _CD)] * 2,
        compiler_params=_params("parallel"),
        name="mem_kv",
    )(mem, g, w_kv)


def _mix_mem_kernel(h_ref, oa_ref, ob_ref, woa_ref, wob_ref, g_ref, wq_ref, mk_ref, mv_ref,
                    wo_ref, o_ref):
    h1 = h_ref[0] + _dot(oa_ref[0], woa_ref[...]) + _dot(ob_ref[0], wob_ref[...])
    xn = _rms(h1, g_ref[...]).astype(_CD)
    q = _dot(xn, wq_ref[...]).astype(_CD)
    hd = q.shape[1] // MEM_HEADS
    heads = []
    for hh in range(MEM_HEADS):
        hs = slice(hh * hd, (hh + 1) * hd)
        s = _dot_nt(q[:, hs], mk_ref[0, :, hs])
        p = jnp.exp(s - jnp.max(s, axis=1, keepdims=True))
        o = _dot(p.astype(_CD), mv_ref[0, :, hs]) / jnp.sum(p, axis=1, keepdims=True)
        heads.append(o.astype(_CD))
    o_ref[0] = h1 + _dot(jnp.concatenate(heads, axis=1), wo_ref[...])


def _mix_mem(h, oa, ob, woa, wob, g, wq, mk, mv, wo):
    b, s, d = h.shape
    tb = TB_MEM
    tok = lambda w: pl.BlockSpec((1, tb, w), lambda bi, j: (bi, j, 0))
    const = lambda a: pl.BlockSpec(a.shape, lambda bi, j: (0,) * a.ndim)
    memspec = pl.BlockSpec((1,) + mk.shape[1:], lambda bi, j: (bi, 0, 0))
    return pl.pallas_call(
        _mix_mem_kernel,
        grid=(b, s // tb),
        in_specs=[tok(d), tok(oa.shape[2]), tok(ob.shape[2]), const(woa), const(wob), const(g),
                  const(wq), memspec, memspec, const(wo)],
        out_specs=tok(d),
        out_shape=jax.ShapeDtypeStruct((b, s, d), _F32),
        compiler_params=_params("parallel", "parallel"),
        name="mix_mem",
    )(h, oa, ob, woa, wob, g, wq, mk, mv, wo)


def _ffn_kernel(h_ref, halo_ref, g_ref, wg_ref, wu_ref, cw_ref, cb_ref, wd_ref, fg_ref, o_ref, *,
                final_norm):
    j = pl.program_id(1)
    tb = h_ref.shape[1]
    g = g_ref[...]
    h = h_ref[0]
    xn = _rms(h, g).astype(_CD)
    xh = _rms(halo_ref[0], g).astype(_CD)
    gate = _dot(jnp.concatenate([xh, xn], axis=0), wg_ref[...])
    keep = jnp.where(j > 0, 1.0, 0.0)
    gate = jnp.concatenate([gate[0:SUBLANES] * keep, gate[SUBLANES:]], axis=0)
    cw = cw_ref[...]
    conv = cb_ref[...] + cw[CONV_WIDTH - 1:CONV_WIDTH] * gate[SUBLANES:SUBLANES + tb]
    for k in range(CONV_WIDTH - 1):
        sh = CONV_WIDTH - 1 - k
        conv = conv + cw[k:k + 1] * gate[SUBLANES - sh:SUBLANES - sh + tb]
    up = _dot(xn, wu_ref[...])
    act = (conv * jax.nn.sigmoid(conv) * up).astype(_CD)
    out = h + _dot(act, wd_ref[...])
    if final_norm:
        out = _rms(out, fg_ref[...])
    o_ref[0] = out


def _ffn(h, g, wg, wu, cw, cb, wd, fg, final_norm):
    b, s, d = h.shape
    tb = TB_FFN
    const = lambda a: pl.BlockSpec(a.shape, lambda bi, j: (0,) * a.ndim)
    per = tb // SUBLANES
    return pl.pallas_call(
        functools.partial(_ffn_kernel, final_norm=final_norm),
        grid=(b, s // tb),
        in_specs=[pl.BlockSpec((1, tb, d), lambda bi, j: (bi, j, 0)),
                  pl.BlockSpec((1, SUBLANES, d), lambda bi, j: (bi, jnp.maximum(j * per - 1, 0), 0)),
                  const(g), const(wg), const(wu), const(cw), const(cb), const(wd), const(fg)],
        out_specs=pl.BlockSpec((1, tb, d), lambda bi, j: (bi, j, 0)),
        out_shape=jax.ShapeDtypeStruct((b, s, d), _F32),
        compiler_params=_params("parallel", "parallel"),
        name="conv_glu",
    )(h, h, g, wg, wu, cw, cb, wd, fg)


def _prep_w_in(w):
    d = w.shape[0]
    o = 0
    parts = {}
    for name, width in (("qa", A_HEADS * HEAD_DIM), ("ka", HEAD_DIM), ("va", HEAD_DIM),
                        ("qi", IDX_HEADS * HEAD_DIM), ("ki", HEAD_DIM), ("wi", IDX_HEADS),
                        ("qb", 2 * B_HEADS * HEAD_DIM), ("kb", 2 * B_HEADS * HEAD_DIM),
                        ("vb", B_HEADS * B_V_DIM)):
        parts[name] = w[:, o:o + width]
        o += width
    scale = HEAD_DIM ** -0.5 * math.log2(math.e)
    z = lambda n: jnp.zeros((d, n), w.dtype)
    cols = [parts["qa"] * scale, parts["qi"], parts["qb"] * scale, parts["kb"],
            parts["ka"], parts["ki"], parts["vb"],
            parts["va"], z(LANES - HEAD_DIM),
            parts["wi"] * (HEAD_DIM ** -0.5 * IDX_HEADS ** -0.5), z(LANES - IDX_HEADS)]
    return jnp.concatenate(cols, axis=1).astype(_CD)


def kernel(x, mem, positions, mix_norm_g, w_in, lambda_q1, lambda_k1, lambda_q2, lambda_k2,
           diff_norm_g, w_out, mem_attn_norm_g, mem_norm_g, w_q_mem, w_kv_mem, w_o_mem, ffn_norm_g,
           w_gate_up, conv_w, conv_b, w_down, final_norm_g):
    b, s, d = x.shape
    depth = w_in.shape[0]
    d_ff = w_down.shape[1]
    n_sel = min(TOPK_MAX, s // 4)
    wa = A_HEADS * HEAD_DIM
    mem_scale = (d // MEM_HEADS) ** -0.5
    row = lambda v: v.reshape(1, -1).astype(_F32)

    rope = _rope_tables(positions)
    h = x
    for l in range(depth):
        qat, qit, qbt, kb, kk, vbt, vat, wit = _in_proj(
            h.reshape(b * s, d), row(mix_norm_g[l]), _prep_w_in(w_in[l]), rope)
        r3 = lambda t: t.reshape(b, s, t.shape[-1])
        out_a = _dsa_attention(qat, qit, wit, r3(kk), vat, n_sel)
        lam_init = 0.8 - 0.6 * math.exp(-0.3 * l)
        lams = jnp.stack([lambda_q1[l], lambda_k1[l], lambda_q2[l], lambda_k2[l]]).astype(_F32)
        out_b = _diff_attention(lams, row(diff_norm_g[l]), qbt, r3(kb), vbt, lam_init)
        mk, mv = _mem_kv(mem, row(mem_norm_g[l]), w_kv_mem[l].astype(_CD))
        h = _mix_mem(h, out_a, out_b, w_out[l, :wa].astype(_CD), w_out[l, wa:].astype(_CD),
                     row(mem_attn_norm_g[l]), (w_q_mem[l] * mem_scale).astype(_CD), mk, mv,
                     w_o_mem[l].astype(_CD))
        h = _ffn(h, row(ffn_norm_g[l]), w_gate_up[l, :, :d_ff].astype(_CD),
                 w_gate_up[l, :, d_ff:].astype(_CD), conv_w[l].astype(_F32), row(conv_b[l]),
                 w_down[l].astype(_CD), row(final_norm_g), l == depth - 1)
    return h
```

```python
import functools
import math

import jax
import jax.numpy as jnp
from jax import lax
from jax.experimental import pallas as pl
from jax.experimental.pallas import tpu as pltpu

A_HEADS = 8
HEAD_DIM = 64
IDX_HEADS = 4
TOPK_MAX = 256
B_HEADS = 4
B_V_DIM = 128
MEM_HEADS = 4
CONV_WIDTH = 3
ROPE_THETA = 500000.0
ROT_DIM = 16
EPS = 1e-6

LANES = 128
SUBLANES = 8
VMEM_LIMIT = 56 * 1024 * 1024

_CD = jnp.bfloat16
_F32 = jnp.float32
_INT_MIN = -(2 ** 31)
_NEG = -1e30

C_QA, C_QI, C_QB, C_KB, C_KK, C_VB, C_VA, C_WI, C_END = (
    0, 512, 768, 1280, 1792, 1920, 2432, 2560, 2688)

TB_PROJ = 512
QB_DSA = 512
KC_DSA = 512
TQ_DIFF = 512
TB_MEM = 512
TB_FFN = 256
COUNT_ROWS = 16
ONES_ROWS = 16
FIELD_BITS = 8
FIELDS_PER_WORD = 4
GUARD_BITS = -(2 ** 31) + 0x00808080
FIELD_ONES = 0x01010101
SEARCH_STAGES = ((25, 7), (18, 7), (11, 7), (4, 7), (0, 4))


def _dot(a, b):
    return jnp.dot(a, b, preferred_element_type=_F32)


def _dot_nt(a, b):
    return lax.dot_general(a, b, (((1,), (1,)), ((), ())), preferred_element_type=_F32)


def _rms(x, g):
    return x * lax.rsqrt(jnp.mean(x * x, axis=-1, keepdims=True) + EPS) * g


def _params(*sem):
    return pltpu.CompilerParams(dimension_semantics=sem, vmem_limit_bytes=VMEM_LIMIT)


def _rope_kernel(pos_ref, freq_ref, o_ref):
    ang = pos_ref[...] * freq_ref[...]
    lane = lax.broadcasted_iota(jnp.int32, ang.shape, 1) & (HEAD_DIM - 1)
    c, s = jnp.cos(ang), jnp.sin(ang)
    half = ROT_DIM // 2
    o_ref[:, 0:LANES] = jnp.where(lane < ROT_DIM, c, 1.0)
    o_ref[:, LANES:2 * LANES] = jnp.where(lane < half, -s, 0.0)
    o_ref[:, 2 * LANES:3 * LANES] = jnp.where((lane >= half) & (lane < ROT_DIM), s, 0.0)


def _rope_tables(positions):
    n = positions.size
    tb = 1024
    half = ROT_DIM // 2
    inv_freq = ROPE_THETA ** (-jnp.arange(0, ROT_DIM, 2, dtype=_F32) / ROT_DIM)
    lane = jnp.arange(LANES) % HEAD_DIM
    freq = jnp.where(lane < ROT_DIM, inv_freq[lane % half], 0.0).astype(_F32)[None, :]
    pos = positions.astype(_F32).reshape(n, 1)
    return pl.pallas_call(
        _rope_kernel,
        grid=(n // tb,),
        in_specs=[pl.BlockSpec((tb, 1), lambda i: (i, 0)),
                  pl.BlockSpec((1, LANES), lambda i: (0, 0))],
        out_specs=pl.BlockSpec((tb, 3 * LANES), lambda i: (i, 0)),
        out_shape=jax.ShapeDtypeStruct((n, 3 * LANES), _F32),
        compiler_params=_params("parallel"),
        name="rope_tables",
    )(pos, freq)


def _in_proj_kernel(h_ref, g_ref, w_ref, rope_ref, qat_ref, qit_ref, qbt_ref, kb_ref, kk_ref,
                    vbt_ref, vat_ref, wit_ref):
    xn = _rms(h_ref[...], g_ref[...]).astype(_CD)
    a = rope_ref[:, 0:LANES]
    bm = rope_ref[:, LANES:2 * LANES]
    bp = rope_ref[:, 2 * LANES:3 * LANES]

    def project(c0, c1, out_ref, transposed):
        p = _dot(xn, w_ref[:, c0:c1])
        for i in range((c1 - c0) // LANES):
            xb = p[:, i * LANES:(i + 1) * LANES]
            xb = (xb * a + pltpu.roll(xb, LANES - ROT_DIM // 2, 1) * bm
                  + pltpu.roll(xb, ROT_DIM // 2, 1) * bp)
            if transposed:
                out_ref[i * LANES:(i + 1) * LANES, :] = xb.T.astype(out_ref.dtype)
            else:
                out_ref[:, i * LANES:(i + 1) * LANES] = xb.astype(out_ref.dtype)

    project(C_QA, C_QI, qat_ref, True)
    project(C_QI, C_QB, qit_ref, True)
    project(C_QB, C_KB, qbt_ref, True)
    project(C_KB, C_KK, kb_ref, False)
    project(C_KK, C_VB, kk_ref, False)
    vbt_ref[...] = _dot(xn, w_ref[:, C_VB:C_VA]).T.astype(vbt_ref.dtype)
    pv = _dot(xn, w_ref[:, C_VA:C_WI])
    lane = lax.broadcasted_iota(jnp.int32, pv.shape, 1)
    vat_ref[...] = jnp.where(lane == HEAD_DIM, 1.0, pv).T.astype(vat_ref.dtype)
    wit_ref[...] = _dot(xn, w_ref[:, C_WI:C_END]).T[0:SUBLANES]


def _in_proj(h2d, g, w, rope):
    n, d = h2d.shape
    tb = TB_PROJ
    tok = lambda wd: (jax.ShapeDtypeStruct((n, wd), _CD), pl.BlockSpec((tb, wd), lambda i: (i, 0)))
    feat = lambda rows, dt=_CD: (jax.ShapeDtypeStruct((rows, n), dt),
                                 pl.BlockSpec((rows, tb), lambda i: (0, i)))
    outs = [feat(C_QI - C_QA), feat(C_QB - C_QI), feat(C_KB - C_QB), tok(C_KK - C_KB),
            tok(C_VB - C_KK), feat(C_VA - C_VB), feat(C_WI - C_VA), feat(SUBLANES, _F32)]
    return pl.pallas_call(
        _in_proj_kernel,
        grid=(n // tb,),
        in_specs=[pl.BlockSpec((tb, d), lambda i: (i, 0)),
                  pl.BlockSpec((1, d), lambda i: (0, 0)),
                  pl.BlockSpec((d, C_END), lambda i: (0, 0)),
                  pl.BlockSpec((tb, 3 * LANES), lambda i: (i, 0))],
        out_specs=[o[1] for o in outs],
        out_shape=[o[0] for o in outs],
        compiler_params=_params("parallel"),
        name="in_proj",
    )(h2d, g, w, rope)


def _dsa_kernel(qat_ref, qit_ref, wit_ref, kk_ref, vat_ref, tri_ref, o_ref,
                keys_ref, pack_ref, acc_ref, sa_stage_ref, sb_stage_ref, *, n_sel, qb, kc):
    j = pl.program_id(1)
    q0 = j * qb
    nc = (q0 + qb + kc - 1) // kc
    hk = kc // 2
    s_total = keys_ref.shape[0]
    zpad = jnp.zeros((HEAD_DIM, qb), _CD)

    qat = qat_ref[...]
    qa_all = jnp.concatenate(
        [jnp.concatenate([qat[h * HEAD_DIM:(h + 1) * HEAD_DIM], zpad], axis=0)
         for h in range(A_HEADS)], axis=1)
    qit = qit_ref[...]
    qi_all = jnp.concatenate(
        [jnp.concatenate([zpad, qit[h * HEAD_DIM:(h + 1) * HEAD_DIM]], axis=0)
         for h in range(IDX_HEADS)], axis=1)

    wi = wit_ref[...]
    row_minus_col = (lax.broadcasted_iota(jnp.int32, (kc, qb), 0)
                     - lax.broadcasted_iota(jnp.int32, (kc, qb), 1))

    def score_body(c, carry):
        off = pl.multiple_of(c * kc, kc)
        d = jnp.maximum(_dot(kk_ref[0, pl.ds(off, kc), :], qi_all), 0.0)
        sc = d[:, 0:qb] * wi[0:1]
        for h in range(1, IDX_HEADS):
            sc = sc + d[:, h * qb:(h + 1) * qb] * wi[h:h + 1]
        bits = lax.bitcast_convert_type(sc, jnp.int32)
        key = bits ^ ((bits >> 31) & 0x7FFFFFFF)
        key = jnp.where(row_minus_col <= q0 - off, key, _INT_MIN)
        keys_ref[pl.ds(off, kc), :] = key
        return carry

    lax.fori_loop(0, nc, score_body, 0)

    def count_ge(cand):
        def body(c, acc):
            off = pl.multiple_of(c * kc, kc)
            ge = jnp.where(keys_ref[pl.ds(off, kc), :] >= cand, 1.0, 0.0)
            return acc + jnp.sum(ge.reshape(kc // COUNT_ROWS, COUNT_ROWS, qb), axis=0)
        acc = lax.fori_loop(0, nc, body, jnp.zeros((COUNT_ROWS, qb), _F32))
        return jnp.sum(acc, axis=0, keepdims=True)

    rpk = kc // FIELDS_PER_WORD

    def pack_stage(field_of):
        def body(c, carry):
            off = pl.multiple_of(c * kc, kc)
            x = None
            for f in range(FIELDS_PER_WORD):
                v = field_of(keys_ref[pl.ds(off + f * rpk, rpk), :])
                v = v if f == 0 else jnp.left_shift(v, FIELD_BITS * f)
                x = v if x is None else x | v
            pack_ref[pl.ds(pl.multiple_of(c * rpk, rpk), rpk), :] = x | GUARD_BITS
            return carry
        lax.fori_loop(0, nc, body, 0)

    def count_fields_ge(cand):
        cand4 = cand * FIELD_ONES

        def body(c, acc):
            off = pl.multiple_of(c * rpk, rpk)
            hit = lax.population_count((pack_ref[pl.ds(off, rpk), :] - cand4) & GUARD_BITS)
            parts = [hit[r * COUNT_ROWS:(r + 1) * COUNT_ROWS] for r in range(rpk // COUNT_ROWS)]
            while len(parts) > 1:
                parts = [parts[i] + parts[i + 1] for i in range(0, len(parts), 2)]
            return acc + parts[0]
        acc = lax.fori_loop(0, nc, body, jnp.zeros((COUNT_ROWS, qb), jnp.int32))
        return jnp.sum(acc.astype(_F32), axis=0, keepdims=True)

    def search_stage(width, n_cur):
        def body(it, carry):
            cur, n_cur = carry
            cand = cur + jnp.left_shift(jnp.int32(1), width - 1 - it)
            n = count_fields_ge(cand)
            ok = n >= n_sel
            return jnp.where(ok, cand, cur), jnp.where(ok, n, n_cur)
        return lax.fori_loop(0, width, body, (jnp.zeros((1, qb), jnp.int32), n_cur))

    thr = jnp.full((1, qb), _INT_MIN, jnp.int32)
    n_ge = jnp.full((1, qb), float(n_sel), _F32)
    prev_shift = 32
    for shift, width in SEARCH_STAGES:
        if prev_shift == 32:
            pack_stage(lambda key, s=shift: lax.shift_right_logical(key ^ _INT_MIN, s))
        else:
            top = thr | ((1 << prev_shift) - 1)
            pack_stage(lambda key, s=shift, lo=thr, hi=top:
                       jnp.right_shift(jnp.minimum(jnp.maximum(key, lo), hi) - lo, s))
        field, n_ge = search_stage(width, n_ge)
        thr = thr ^ jnp.left_shift(field, shift)
        prev_shift = shift
    floor_row = thr == _INT_MIN
    has_ties = jnp.max(jnp.where((n_ge > n_sel) & jnp.logical_not(floor_row), 1.0, 0.0)) > 0.0
    thr_all_equal = jnp.where(floor_row, _INT_MIN + 1, thr)

    acc_ref[...] = jnp.zeros_like(acc_ref)

    @pl.when(has_ties)
    def _():
        need = jnp.where(floor_row, 0.0, n_sel - count_ge(thr + 1))

        def retire_body(c, eq_seen):
            off = pl.multiple_of(c * kc, kc)
            key = keys_ref[pl.ds(off, kc), :]
            eq = key == thr
            eqf = jnp.where(eq, 1.0, 0.0)
            rank = _dot(tri_ref[...], eqf.astype(_CD)) + eq_seen
            keys_ref[pl.ds(off, kc), :] = jnp.where(eq & (rank >= need), _INT_MIN, key)
            return eq_seen + jnp.sum(eqf, axis=0, keepdims=True)
        lax.fori_loop(0, nc, retire_body, jnp.zeros((1, qb), _F32))

    def qk(row0, s_ref):
        s_ref[...] = _dot(kk_ref[0, pl.ds(row0, hk), :], qa_all)

    def consume(row0, s_ref, m):
        bias = jnp.where(keys_ref[pl.ds(row0, hk), :] >= thr_all_equal, 0.0, _NEG)
        ps, m_news = [], []
        for h in range(A_HEADS):
            cs = slice(h * qb, (h + 1) * qb)
            s = s_ref[:, cs] + bias
            m_new = jnp.maximum(m[:, cs], jnp.max(s, axis=0, keepdims=True))
            ps.append(jnp.exp2(s - m_new).astype(_CD))
            m_news.append(m_new)
        m_new = jnp.concatenate(m_news, axis=1)
        pv = _dot(vat_ref[:, pl.ds(row0, hk)], jnp.concatenate(ps, axis=1))
        acc_ref[...] = jnp.exp2(m - m_new) * acc_ref[...] + pv
        return m_new

    qk(0, sa_stage_ref)

    def attn_body(c, m):
        off = pl.multiple_of(c * kc, kc)
        qk(off + hk, sb_stage_ref)
        m = consume(off, sa_stage_ref, m)
        qk(pl.multiple_of(jnp.minimum(off + kc, s_total - hk), hk), sa_stage_ref)
        return consume(off + hk, sb_stage_ref, m)

    lax.fori_loop(0, nc, attn_body, jnp.full((1, A_HEADS * qb), _NEG, _F32))

    acc = acc_ref[...]
    out_t = acc / acc[HEAD_DIM:HEAD_DIM + 1, :]
    low = lax.broadcasted_iota(jnp.int32, (qb, LANES), 1) < HEAD_DIM
    for p in range(A_HEADS // 2):
        even = out_t[:, (2 * p) * qb:(2 * p + 1) * qb].T
        odd = out_t[:, (2 * p + 1) * qb:(2 * p + 2) * qb].T
        blk = jnp.where(low, even, pltpu.roll(odd, HEAD_DIM, 1))
        o_ref[0, :, p * LANES:(p + 1) * LANES] = blk.astype(o_ref.dtype)


def _dsa_attention(qat, qit, wit, kk, vat, n_sel):
    b, s, _ = kk.shape
    qb, kc = QB_DSA, KC_DSA
    nq = s // qb
    tri = jnp.tril(jnp.ones((kc, kc), _F32), k=-1).astype(_CD)
    qspec = lambda rows: pl.BlockSpec((rows, qb), lambda bi, j: (0, bi * nq + j))
    return pl.pallas_call(
        functools.partial(_dsa_kernel, n_sel=n_sel, qb=qb, kc=kc),
        grid=(b, nq),
        in_specs=[qspec(qat.shape[0]), qspec(qit.shape[0]), qspec(SUBLANES),
                  pl.BlockSpec((1, s, LANES), lambda bi, j: (bi, 0, 0)),
                  pl.BlockSpec((LANES, s), lambda bi, j: (0, bi)),
                  pl.BlockSpec((kc, kc), lambda bi, j: (0, 0))],
        out_specs=pl.BlockSpec((1, qb, A_HEADS * HEAD_DIM), lambda bi, j: (bi, j, 0)),
        out_shape=jax.ShapeDtypeStruct((b, s, A_HEADS * HEAD_DIM), _CD),
        scratch_shapes=[pltpu.VMEM((s, qb), jnp.int32),
                        pltpu.VMEM((s // FIELDS_PER_WORD, qb), jnp.int32),
                        pltpu.VMEM((LANES, A_HEADS * qb), _F32),
                        pltpu.VMEM((kc // 2, A_HEADS * qb), _F32),
                        pltpu.VMEM((kc // 2, A_HEADS * qb), _F32)],
        compiler_params=_params("parallel", "arbitrary"),
        name="dsa_attention",
    )(qat, qit, wit, kk, vat, tri)


def _diff_kernel(lam_ref, g_ref, qt_ref, k_ref, vt_ref, o_ref, acc_ref, sa_stage_ref, sb_stage_ref,
                 *, lam_init, tq):
    j = pl.program_id(1)
    hk = tq // 2
    lv = lam_ref[...]
    lam = (jnp.exp(jnp.sum(lv[0:1] * lv[1:2], axis=1, keepdims=True))
           - jnp.exp(jnp.sum(lv[2:3] * lv[3:4], axis=1, keepdims=True)) + lam_init)
    zpad = jnp.zeros((HEAD_DIM, tq), _CD)
    ones_rows = jnp.ones((ONES_ROWS, hk), _CD)
    krow = lax.broadcasted_iota(jnp.int32, (hk, 2 * tq), 0)
    qcol = lax.broadcasted_iota(jnp.int32, (hk, 2 * tq), 1)
    qidx = jnp.where(qcol >= tq, qcol - tq, qcol)
    g = g_ref[...] * (1.0 - lam_init)

    for h in range(B_HEADS):
        hs = slice(h * LANES, (h + 1) * LANES)
        qt = qt_ref[hs, :]
        q_all = jnp.concatenate([jnp.concatenate([qt[0:HEAD_DIM], zpad], axis=0),
                                 jnp.concatenate([zpad, qt[HEAD_DIM:]], axis=0)], axis=1)
        acc_ref[...] = jnp.zeros_like(acc_ref)

        def qk(row0, s_ref, hs=hs, q_all=q_all):
            s_ref[...] = _dot(k_ref[0, pl.ds(row0, hk), hs], q_all)

        def consume(row0, s_ref, m, diag_row0=None, hs=hs):
            s = s_ref[...]
            if diag_row0 is not None:
                s = jnp.where(krow + diag_row0 <= qidx, s, _NEG)
            m_new = jnp.maximum(m, jnp.max(s, axis=0, keepdims=True))
            p = jnp.exp2(s - m_new)
            vt_ext = jnp.concatenate([vt_ref[hs, pl.ds(row0, hk)], ones_rows], axis=0)
            acc_ref[...] = jnp.exp2(m - m_new) * acc_ref[...] + _dot(vt_ext, p.astype(_CD))
            return m_new

        qk(0, sa_stage_ref)

        def body(c, carry, qk=qk, consume=consume):
            off = pl.multiple_of(c * tq, tq)
            qk(off + hk, sb_stage_ref)
            carry = consume(off, sa_stage_ref, carry)
            qk(off + tq, sa_stage_ref)
            return consume(off + hk, sb_stage_ref, carry)

        carry = lax.fori_loop(0, j, body, jnp.full((1, 2 * tq), _NEG, _F32))
        off = pl.multiple_of(j * tq, tq)
        qk(off + hk, sb_stage_ref)
        carry = consume(off, sa_stage_ref, carry, diag_row0=0)
        consume(off + hk, sb_stage_ref, carry, diag_row0=hk)
        acc = acc_ref[...]
        o_t = acc[0:LANES] / acc[LANES:LANES + 1]
        d = (o_t[:, 0:tq] - lam * o_t[:, tq:2 * tq]).T
        o_ref[0, :, hs] = _rms(d, g).astype(o_ref.dtype)


def _diff_attention(lams, g, qt, k, vt, lam_init):
    b, s, w = k.shape
    tq = TQ_DIFF
    nq = s // tq
    return pl.pallas_call(
        functools.partial(_diff_kernel, lam_init=lam_init, tq=tq),
        grid=(b, nq),
        in_specs=[pl.BlockSpec(lams.shape, lambda bi, j: (0, 0)),
                  pl.BlockSpec(g.shape, lambda bi, j: (0, 0)),
                  pl.BlockSpec((w, tq), lambda bi, j: (0, bi * nq + j)),
                  pl.BlockSpec((1, s, w), lambda bi, j: (bi, 0, 0)),
                  pl.BlockSpec((w, s), lambda bi, j: (0, bi))],
        out_specs=pl.BlockSpec((1, tq, w), lambda bi, j: (bi, j, 0)),
        out_shape=jax.ShapeDtypeStruct((b, s, w), _CD),
        scratch_shapes=[pltpu.VMEM((LANES + ONES_ROWS, 2 * tq), _F32),
                        pltpu.VMEM((tq // 2, 2 * tq), _F32),
                        pltpu.VMEM((tq // 2, 2 * tq), _F32)],
        compiler_params=_params("parallel", "arbitrary"),
        name="diff_attention",
    )(lams, g, qt, k, vt)


def _mem_kv_kernel(mem_ref, g_ref, w_ref, k_ref, v_ref):
    d = mem_ref.shape[2]
    mn = _rms(mem_ref[0], g_ref[...]).astype(_CD)
    k_ref[0] = _dot(mn, w_ref[:, 0:d]).astype(k_ref.dtype)
    v_ref[0] = _dot(mn, w_ref[:, d:2 * d]).astype(v_ref.dtype)


def _mem_kv(mem, g, w_kv):
    b, m, d = mem.shape
    blk = pl.BlockSpec((1, m, d), lambda bi: (bi, 0, 0))
    return pl.pallas_call(
        _mem_kv_kernel,
        grid=(b,),
        in_specs=[blk, pl.BlockSpec((1, d), lambda bi: (0, 0)),
                  pl.BlockSpec((d, 2 * d), lambda bi: (0, 0))],
        out_specs=[blk, blk],
        out_shape=[jax.ShapeDtypeStruct((b, m, d), _CD)] * 2,
        compiler_params=_params("parallel"),
        name="mem_kv",
    )(mem, g, w_kv)


def _mix_mem_kernel(h_ref, oa_ref, ob_ref, woa_ref, wob_ref, g_ref, wq_ref, mk_ref, mv_ref,
                    wo_ref, o_ref):
    h1 = h_ref[0] + _dot(oa_ref[0], woa_ref[...]) + _dot(ob_ref[0], wob_ref[...])
    xn = _rms(h1, g_ref[...]).astype(_CD)
    q = _dot(xn, wq_ref[...]).astype(_CD)
    hd = q.shape[1] // MEM_HEADS
    heads = []
    for hh in range(MEM_HEADS):
        hs = slice(hh * hd, (hh + 1) * hd)
        s = _dot_nt(q[:, hs], mk_ref[0, :, hs])
        p = jnp.exp(s - jnp.max(s, axis=1, keepdims=True))
        o = _dot(p.astype(_CD), mv_ref[0, :, hs]) / jnp.sum(p, axis=1, keepdims=True)
        heads.append(o.astype(_CD))
    o_ref[0] = h1 + _dot(jnp.concatenate(heads, axis=1), wo_ref[...])


def _mix_mem(h, oa, ob, woa, wob, g, wq, mk, mv, wo):
    b, s, d = h.shape
    tb = TB_MEM
    tok = lambda w: pl.BlockSpec((1, tb, w), lambda bi, j: (bi, j, 0))
    const = lambda a: pl.BlockSpec(a.shape, lambda bi, j: (0,) * a.ndim)
    memspec = pl.BlockSpec((1,) + mk.shape[1:], lambda bi, j: (bi, 0, 0))
    return pl.pallas_call(
        _mix_mem_kernel,
        grid=(b, s // tb),
        in_specs=[tok(d), tok(oa.shape[2]), tok(ob.shape[2]), const(woa), const(wob), const(g),
                  const(wq), memspec, memspec, const(wo)],
        out_specs=tok(d),
        out_shape=jax.ShapeDtypeStruct((b, s, d), _F32),
        compiler_params=_params("parallel", "parallel"),
        name="mix_mem",
    )(h, oa, ob, woa, wob, g, wq, mk, mv, wo)


def _ffn_kernel(h_ref, halo_ref, g_ref, wg_ref, wu_ref, cw_ref, cb_ref, wd_ref, fg_ref, o_ref, *,
                final_norm):
    j = pl.program_id(1)
    tb = h_ref.shape[1]
    g = g_ref[...]
    h = h_ref[0]
    xn = _rms(h, g).astype(_CD)
    xh = _rms(halo_ref[0], g).astype(_CD)
    gate = _dot(jnp.concatenate([xh, xn], axis=0), wg_ref[...])
    keep = jnp.where(j > 0, 1.0, 0.0)
    gate = jnp.concatenate([gate[0:SUBLANES] * keep, gate[SUBLANES:]], axis=0)
    cw = cw_ref[...]
    conv = cb_ref[...] + cw[CONV_WIDTH - 1:CONV_WIDTH] * gate[SUBLANES:SUBLANES + tb]
    for k in range(CONV_WIDTH - 1):
        sh = CONV_WIDTH - 1 - k
        conv = conv + cw[k:k + 1] * gate[SUBLANES - sh:SUBLANES - sh + tb]
    up = _dot(xn, wu_ref[...])
    act = (conv * jax.nn.sigmoid(conv) * up).astype(_CD)
    out = h + _dot(act, wd_ref[...])
    if final_norm:
        out = _rms(out, fg_ref[...])
    o_ref[0] = out


def _ffn(h, g, wg, wu, cw, cb, wd, fg, final_norm):
    b, s, d = h.shape
    tb = TB_FFN
    const = lambda a: pl.BlockSpec(a.shape, lambda bi, j: (0,) * a.ndim)
    per = tb // SUBLANES
    return pl.pallas_call(
        functools.partial(_ffn_kernel, final_norm=final_norm),
        grid=(b, s // tb),
        in_specs=[pl.BlockSpec((1, tb, d), lambda bi, j: (bi, j, 0)),
                  pl.BlockSpec((1, SUBLANES, d), lambda bi, j: (bi, jnp.maximum(j * per - 1, 0), 0)),
                  const(g), const(wg), const(wu), const(cw), const(cb), const(wd), const(fg)],
        out_specs=pl.BlockSpec((1, tb, d), lambda bi, j: (bi, j, 0)),
        out_shape=jax.ShapeDtypeStruct((b, s, d), _F32),
        compiler_params=_params("parallel", "parallel"),
        name="conv_glu",
    )(h, h, g, wg, wu, cw, cb, wd, fg)


def _prep_w_in(w):
    d = w.shape[0]
    o = 0
    parts = {}
    for name, width in (("qa", A_HEADS * HEAD_DIM), ("ka", HEAD_DIM), ("va", HEAD_DIM),
                        ("qi", IDX_HEADS * HEAD_DIM), ("ki", HEAD_DIM), ("wi", IDX_HEADS),
                        ("qb", 2 * B_HEADS * HEAD_DIM), ("kb", 2 * B_HEADS * HEAD_DIM),
                        ("vb", B_HEADS * B_V_DIM)):
        parts[name] = w[:, o:o + width]
        o += width
    scale = HEAD_DIM ** -0.5 * math.log2(math.e)
    z = lambda n: jnp.zeros((d, n), w.dtype)
    cols = [parts["qa"] * scale, parts["qi"], parts["qb"] * scale, parts["kb"],
            parts["ka"], parts["ki"], parts["vb"],
            parts["va"], z(LANES - HEAD_DIM),
            parts["wi"] * (HEAD_DIM ** -0.5 * IDX_HEADS ** -0.5), z(LANES - IDX_HEADS)]
    return jnp.concatenate(cols, axis=1).astype(_CD)


def kernel(x, mem, positions, mix_norm_g, w_in, lambda_q1, lambda_k1, lambda_q2, lambda_k2,
           diff_norm_g, w_out, mem_attn_norm_g, mem_norm_g, w_q_mem, w_kv_mem, w_o_mem, ffn_norm_g,
           w_gate_up, conv_w, conv_b, w_down, final_norm_g):
    b, s, d = x.shape
    depth = w_in.shape[0]
    d_ff = w_down.shape[1]
    n_sel = min(TOPK_MAX, s // 4)
    wa = A_HEADS * HEAD_DIM
    mem_scale = (d // MEM_HEADS) ** -0.5
    row = lambda v: v.reshape(1, -1).astype(_F32)

    rope = _rope_tables(positions)
    h = x
    for l in range(depth):
        qat, qit, qbt, kb, kk, vbt, vat, wit = _in_proj(
            h.reshape(b * s, d), row(mix_norm_g[l]), _prep_w_in(w_in[l]), rope)
        r3 = lambda t: t.reshape(b, s, t.shape[-1])
        out_a = _dsa_attention(qat, qit, wit, r3(kk), vat, n_sel)
        lam_init = 0.8 - 0.6 * math.exp(-0.3 * l)
        lams = jnp.stack([lambda_q1[l], lambda_k1[l], lambda_q2[l], lambda_k2[l]]).astype(_F32)
        out_b = _diff_attention(lams, row(diff_norm_g[l]), qbt, r3(kb), vbt, lam_init)
        mk, mv = _mem_kv(mem, row(mem_norm_g[l]), w_kv_mem[l].astype(_CD))
        h = _mix_mem(h, out_a, out_b, w_out[l, :wa].astype(_CD), w_out[l, wa:].astype(_CD),
                     row(mem_attn_norm_g[l]), (w_q_mem[l] * mem_scale).astype(_CD), mk, mv,
                     w_o_mem[l].astype(_CD))
        h = _ffn(h, row(ffn_norm_g[l]), w_gate_up[l, :, :d_ff].astype(_CD),
                 w_gate_up[l, :, d_ff:].astype(_CD), conv_w[l].astype(_F32), row(conv_b[l]),
                 w_down[l].astype(_CD), row(final_norm_g), l == depth - 1)
    return h
```

```python
import functools
import math

import jax
import jax.numpy as jnp
from jax import lax
from jax.experimental import pallas as pl
from jax.experimental.pallas import tpu as pltpu

A_HEADS = 8
HEAD_DIM = 64
IDX_HEADS = 4
TOPK_MAX = 256
B_HEADS = 4
B_V_DIM = 128
MEM_HEADS = 4
CONV_WIDTH = 3
ROPE_THETA = 500000.0
ROT_DIM = 16
EPS = 1e-6

LANES = 128
SUBLANES = 8
VMEM_LIMIT = 56 * 1024 * 1024

_CD = jnp.bfloat16
_F32 = jnp.float32
_INT_MIN = -(2 ** 31)
_NEG = -1e30

C_QA, C_QI, C_QB, C_KB, C_KK, C_VB, C_VA, C_WI, C_END = (
    0, 512, 768, 1280, 1792, 1920, 2432, 2560, 2688)

TB_PROJ = 512
QB_DSA = 512
KC_DSA = 512
TQ_DIFF = 1024
TB_MEM = 512
TB_FFN = 256
COUNT_ROWS = 16
ONES_ROWS = 16
FIELD_BITS = 8
FIELDS_PER_WORD = 4
GUARD_BITS = -(2 ** 31) + 0x00808080
FIELD_ONES = 0x01010101
SEARCH_STAGES = ((25, 7), (18, 7), (11, 7), (4, 7), (0, 4))


def _dot(a, b):
    return jnp.dot(a, b, preferred_element_type=_F32)


def _dot_nt(a, b):
    return lax.dot_general(a, b, (((1,), (1,)), ((), ())), preferred_element_type=_F32)


def _rms(x, g):
    return x * lax.rsqrt(jnp.mean(x * x, axis=-1, keepdims=True) + EPS) * g


def _params(*sem):
    return pltpu.CompilerParams(dimension_semantics=sem, vmem_limit_bytes=VMEM_LIMIT)


def _rope_kernel(pos_ref, freq_ref, o_ref):
    ang = pos_ref[...] * freq_ref[...]
    lane = lax.broadcasted_iota(jnp.int32, ang.shape, 1) & (HEAD_DIM - 1)
    c, s = jnp.cos(ang), jnp.sin(ang)
    half = ROT_DIM // 2
    o_ref[:, 0:LANES] = jnp.where(lane < ROT_DIM, c, 1.0)
    o_ref[:, LANES:2 * LANES] = jnp.where(lane < half, -s, 0.0)
    o_ref[:, 2 * LANES:3 * LANES] = jnp.where((lane >= half) & (lane < ROT_DIM), s, 0.0)


def _rope_tables(positions):
    n = positions.size
    tb = 1024
    half = ROT_DIM // 2
    inv_freq = ROPE_THETA ** (-jnp.arange(0, ROT_DIM, 2, dtype=_F32) / ROT_DIM)
    lane = jnp.arange(LANES) % HEAD_DIM
    freq = jnp.where(lane < ROT_DIM, inv_freq[lane % half], 0.0).astype(_F32)[None, :]
    pos = positions.astype(_F32).reshape(n, 1)
    return pl.pallas_call(
        _rope_kernel,
        grid=(n // tb,),
        in_specs=[pl.BlockSpec((tb, 1), lambda i: (i, 0)),
                  pl.BlockSpec((1, LANES), lambda i: (0, 0))],
        out_specs=pl.BlockSpec((tb, 3 * LANES), lambda i: (i, 0)),
        out_shape=jax.ShapeDtypeStruct((n, 3 * LANES), _F32),
        compiler_params=_params("parallel"),
        name="rope_tables",
    )(pos, freq)


def _in_proj_kernel(h_ref, g_ref, w_ref, rope_ref, qat_ref, qit_ref, qbt_ref, kb_ref, kk_ref,
                    vbt_ref, vat_ref, wit_ref):
    xn = _rms(h_ref[...], g_ref[...]).astype(_CD)
    a = rope_ref[:, 0:LANES]
    bm = rope_ref[:, LANES:2 * LANES]
    bp = rope_ref[:, 2 * LANES:3 * LANES]

    def project(c0, c1, out_ref, transposed):
        p = _dot(xn, w_ref[:, c0:c1])
        for i in range((c1 - c0) // LANES):
            xb = p[:, i * LANES:(i + 1) * LANES]
            xb = (xb * a + pltpu.roll(xb, LANES - ROT_DIM // 2, 1) * bm
                  + pltpu.roll(xb, ROT_DIM // 2, 1) * bp)
            if transposed:
                out_ref[i * LANES:(i + 1) * LANES, :] = xb.T.astype(out_ref.dtype)
            else:
                out_ref[:, i * LANES:(i + 1) * LANES] = xb.astype(out_ref.dtype)

    project(C_QA, C_QI, qat_ref, True)
    project(C_QI, C_QB, qit_ref, True)
    project(C_QB, C_KB, qbt_ref, True)
    project(C_KB, C_KK, kb_ref, False)
    project(C_KK, C_VB, kk_ref, False)
    vbt_ref[...] = _dot(xn, w_ref[:, C_VB:C_VA]).T.astype(vbt_ref.dtype)
    pv = _dot(xn, w_ref[:, C_VA:C_WI])
    lane = lax.broadcasted_iota(jnp.int32, pv.shape, 1)
    vat_ref[...] = jnp.where(lane == HEAD_DIM, 1.0, pv).T.astype(vat_ref.dtype)
    wit_ref[...] = _dot(xn, w_ref[:, C_WI:C_END]).T[0:SUBLANES]


def _in_proj(h2d, g, w, rope):
    n, d = h2d.shape
    tb = TB_PROJ
    tok = lambda wd: (jax.ShapeDtypeStruct((n, wd), _CD), pl.BlockSpec((tb, wd), lambda i: (i, 0)))
    feat = lambda rows, dt=_CD: (jax.ShapeDtypeStruct((rows, n), dt),
                                 pl.BlockSpec((rows, tb), lambda i: (0, i)))
    outs = [feat(C_QI - C_QA), feat(C_QB - C_QI), feat(C_KB - C_QB), tok(C_KK - C_KB),
            tok(C_VB - C_KK), feat(C_VA - C_VB), feat(C_WI - C_VA), feat(SUBLANES, _F32)]
    return pl.pallas_call(
        _in_proj_kernel,
        grid=(n // tb,),
        in_specs=[pl.BlockSpec((tb, d), lambda i: (i, 0)),
                  pl.BlockSpec((1, d), lambda i: (0, 0)),
                  pl.BlockSpec((d, C_END), lambda i: (0, 0)),
                  pl.BlockSpec((tb, 3 * LANES), lambda i: (i, 0))],
        out_specs=[o[1] for o in outs],
        out_shape=[o[0] for o in outs],
        compiler_params=_params("parallel"),
        name="in_proj",
    )(h2d, g, w, rope)


def _dsa_kernel(qat_ref, qit_ref, wit_ref, kk_ref, vat_ref, tri_ref, o_ref,
                keys_ref, pack_ref, acc_ref, sa_stage_ref, sb_stage_ref, *, n_sel, qb, kc):
    j = pl.program_id(1)
    q0 = j * qb
    nc = (q0 + qb + kc - 1) // kc
    hk = kc // 2
    s_total = keys_ref.shape[0]
    zpad = jnp.zeros((HEAD_DIM, qb), _CD)

    qat = qat_ref[...]
    qa_all = jnp.concatenate(
        [jnp.concatenate([qat[h * HEAD_DIM:(h + 1) * HEAD_DIM], zpad], axis=0)
         for h in range(A_HEADS)], axis=1)
    qit = qit_ref[...]
    qi_all = jnp.concatenate(
        [jnp.concatenate([zpad, qit[h * HEAD_DIM:(h + 1) * HEAD_DIM]], axis=0)
         for h in range(IDX_HEADS)], axis=1)

    wi = wit_ref[...]
    row_minus_col = (lax.broadcasted_iota(jnp.int32, (kc, qb), 0)
                     - lax.broadcasted_iota(jnp.int32, (kc, qb), 1))

    def score_body(c, carry):
        off = pl.multiple_of(c * kc, kc)
        d = jnp.maximum(_dot(kk_ref[0, pl.ds(off, kc), :], qi_all), 0.0)
        sc = d[:, 0:qb] * wi[0:1]
        for h in range(1, IDX_HEADS):
            sc = sc + d[:, h * qb:(h + 1) * qb] * wi[h:h + 1]
        bits = lax.bitcast_convert_type(sc, jnp.int32)
        key = bits ^ ((bits >> 31) & 0x7FFFFFFF)
        key = jnp.where(row_minus_col <= q0 - off, key, _INT_MIN)
        keys_ref[pl.ds(off, kc), :] = key
        return carry

    lax.fori_loop(0, nc, score_body, 0)

    def count_ge(cand):
        def body(c, acc):
            off = pl.multiple_of(c * kc, kc)
            ge = jnp.where(keys_ref[pl.ds(off, kc), :] >= cand, 1.0, 0.0)
            return acc + jnp.sum(ge.reshape(kc // COUNT_ROWS, COUNT_ROWS, qb), axis=0)
        acc = lax.fori_loop(0, nc, body, jnp.zeros((COUNT_ROWS, qb), _F32))
        return jnp.sum(acc, axis=0, keepdims=True)

    rpk = kc // FIELDS_PER_WORD

    def pack_stage(field_of):
        def body(c, carry):
            off = pl.multiple_of(c * kc, kc)
            x = None
            for f in range(FIELDS_PER_WORD):
                v = field_of(keys_ref[pl.ds(off + f * rpk, rpk), :])
                v = v if f == 0 else jnp.left_shift(v, FIELD_BITS * f)
                x = v if x is None else x | v
            pack_ref[pl.ds(pl.multiple_of(c * rpk, rpk), rpk), :] = x | GUARD_BITS
            return carry
        lax.fori_loop(0, nc, body, 0)

    def count_fields_ge(cand):
        cand4 = cand * FIELD_ONES

        def body(c, acc):
            off = pl.multiple_of(c * rpk, rpk)
            hit = lax.population_count((pack_ref[pl.ds(off, rpk), :] - cand4) & GUARD_BITS)
            parts = [hit[r * COUNT_ROWS:(r + 1) * COUNT_ROWS] for r in range(rpk // COUNT_ROWS)]
            while len(parts) > 1:
                parts = [parts[i] + parts[i + 1] for i in range(0, len(parts), 2)]
            return acc + parts[0]
        acc = lax.fori_loop(0, nc, body, jnp.zeros((COUNT_ROWS, qb), jnp.int32))
        return jnp.sum(acc.astype(_F32), axis=0, keepdims=True)

    def search_stage(width, n_cur):
        def body(it, carry):
            cur, n_cur = carry
            cand = cur + jnp.left_shift(jnp.int32(1), width - 1 - it)
            n = count_fields_ge(cand)
            ok = n >= n_sel
            return jnp.where(ok, cand, cur), jnp.where(ok, n, n_cur)
        return lax.fori_loop(0, width, body, (jnp.zeros((1, qb), jnp.int32), n_cur))

    thr = jnp.full((1, qb), _INT_MIN, jnp.int32)
    n_ge = jnp.full((1, qb), float(n_sel), _F32)
    prev_shift = 32
    for shift, width in SEARCH_STAGES:
        if prev_shift == 32:
            pack_stage(lambda key, s=shift: lax.shift_right_logical(key ^ _INT_MIN, s))
        else:
            top = thr | ((1 << prev_shift) - 1)
            pack_stage(lambda key, s=shift, lo=thr, hi=top:
                       jnp.right_shift(jnp.minimum(jnp.maximum(key, lo), hi) - lo, s))
        field, n_ge = search_stage(width, n_ge)
        thr = thr ^ jnp.left_shift(field, shift)
        prev_shift = shift
    floor_row = thr == _INT_MIN
    has_ties = jnp.max(jnp.where((n_ge > n_sel) & jnp.logical_not(floor_row), 1.0, 0.0)) > 0.0
    thr_all_equal = jnp.where(floor_row, _INT_MIN + 1, thr)

    acc_ref[...] = jnp.zeros_like(acc_ref)

    @pl.when(has_ties)
    def _():
        need = jnp.where(floor_row, 0.0, n_sel - count_ge(thr + 1))

        def retire_body(c, eq_seen):
            off = pl.multiple_of(c * kc, kc)
            key = keys_ref[pl.ds(off, kc), :]
            eq = key == thr
            eqf = jnp.where(eq, 1.0, 0.0)
            rank = _dot(tri_ref[...], eqf.astype(_CD)) + eq_seen
            keys_ref[pl.ds(off, kc), :] = jnp.where(eq & (rank >= need), _INT_MIN, key)
            return eq_seen + jnp.sum(eqf, axis=0, keepdims=True)
        lax.fori_loop(0, nc, retire_body, jnp.zeros((1, qb), _F32))

    def qk(row0, s_ref):
        s_ref[...] = _dot(kk_ref[0, pl.ds(row0, hk), :], qa_all)

    def consume(row0, s_ref, m):
        bias = jnp.where(keys_ref[pl.ds(row0, hk), :] >= thr_all_equal, 0.0, _NEG)
        ps, m_news = [], []
        for h in range(A_HEADS):
            cs = slice(h * qb, (h + 1) * qb)
            s = s_ref[:, cs] + bias
            m_new = jnp.maximum(m[:, cs], jnp.max(s, axis=0, keepdims=True))
            ps.append(jnp.exp2(s - m_new).astype(_CD))
            m_news.append(m_new)
        m_new = jnp.concatenate(m_news, axis=1)
        pv = _dot(vat_ref[:, pl.ds(row0, hk)], jnp.concatenate(ps, axis=1))
        acc_ref[...] = jnp.exp2(m - m_new) * acc_ref[...] + pv
        return m_new

    qk(0, sa_stage_ref)

    def attn_body(c, m):
        off = pl.multiple_of(c * kc, kc)
        qk(off + hk, sb_stage_ref)
        m = consume(off, sa_stage_ref, m)
        qk(pl.multiple_of(jnp.minimum(off + kc, s_total - hk), hk), sa_stage_ref)
        return consume(off + hk, sb_stage_ref, m)

    lax.fori_loop(0, nc, attn_body, jnp.full((1, A_HEADS * qb), _NEG, _F32))

    acc = acc_ref[...]
    out_t = acc / acc[HEAD_DIM:HEAD_DIM + 1, :]
    low = lax.broadcasted_iota(jnp.int32, (qb, LANES), 1) < HEAD_DIM
    for p in range(A_HEADS // 2):
        even = out_t[:, (2 * p) * qb:(2 * p + 1) * qb].T
        odd = out_t[:, (2 * p + 1) * qb:(2 * p + 2) * qb].T
        blk = jnp.where(low, even, pltpu.roll(odd, HEAD_DIM, 1))
        o_ref[0, :, p * LANES:(p + 1) * LANES] = blk.astype(o_ref.dtype)


def _dsa_attention(qat, qit, wit, kk, vat, n_sel):
    b, s, _ = kk.shape
    qb, kc = QB_DSA, KC_DSA
    nq = s // qb
    tri = jnp.tril(jnp.ones((kc, kc), _F32), k=-1).astype(_CD)
    qspec = lambda rows: pl.BlockSpec((rows, qb), lambda bi, j: (0, bi * nq + j))
    return pl.pallas_call(
        functools.partial(_dsa_kernel, n_sel=n_sel, qb=qb, kc=kc),
        grid=(b, nq),
        in_specs=[qspec(qat.shape[0]), qspec(qit.shape[0]), qspec(SUBLANES),
                  pl.BlockSpec((1, s, LANES), lambda bi, j: (bi, 0, 0)),
                  pl.BlockSpec((LANES, s), lambda bi, j: (0, bi)),
                  pl.BlockSpec((kc, kc), lambda bi, j: (0, 0))],
        out_specs=pl.BlockSpec((1, qb, A_HEADS * HEAD_DIM), lambda bi, j: (bi, j, 0)),
        out_shape=jax.ShapeDtypeStruct((b, s, A_HEADS * HEAD_DIM), _CD),
        scratch_shapes=[pltpu.VMEM((s, qb), jnp.int32),
                        pltpu.VMEM((s // FIELDS_PER_WORD, qb), jnp.int32),
                        pltpu.VMEM((LANES, A_HEADS * qb), _F32),
                        pltpu.VMEM((kc // 2, A_HEADS * qb), _F32),
                        pltpu.VMEM((kc // 2, A_HEADS * qb), _F32)],
        compiler_params=_params("parallel", "arbitrary"),
        name="dsa_attention",
    )(qat, qit, wit, kk, vat, tri)


def _diff_kernel(lam_ref, g_ref, qt_ref, k_ref, vt_ref, o_ref, acc_ref, sa_stage_ref, sb_stage_ref,
                 *, lam_init, tq):
    j = pl.program_id(1)
    hk = tq // 2
    lv = lam_ref[...]
    lam = (jnp.exp(jnp.sum(lv[0:1] * lv[1:2], axis=1, keepdims=True))
           - jnp.exp(jnp.sum(lv[2:3] * lv[3:4], axis=1, keepdims=True)) + lam_init)
    zpad = jnp.zeros((HEAD_DIM, tq), _CD)
    ones_rows = jnp.ones((ONES_ROWS, hk), _CD)
    krow = lax.broadcasted_iota(jnp.int32, (hk, 2 * tq), 0)
    qcol = lax.broadcasted_iota(jnp.int32, (hk, 2 * tq), 1)
    qidx = jnp.where(qcol >= tq, qcol - tq, qcol)
    g = g_ref[...] * (1.0 - lam_init)

    for h in range(B_HEADS):
        hs = slice(h * LANES, (h + 1) * LANES)
        qt = qt_ref[hs, :]
        q_all = jnp.concatenate([jnp.concatenate([qt[0:HEAD_DIM], zpad], axis=0),
                                 jnp.concatenate([zpad, qt[HEAD_DIM:]], axis=0)], axis=1)
        acc_ref[...] = jnp.zeros_like(acc_ref)

        def qk(row0, s_ref, hs=hs, q_all=q_all):
            s_ref[...] = _dot(k_ref[0, pl.ds(row0, hk), hs], q_all)

        def consume(row0, s_ref, m, diag_row0=None, hs=hs):
            s = s_ref[...]
            if diag_row0 is not None:
                s = jnp.where(krow + diag_row0 <= qidx, s, _NEG)
            m_new = jnp.maximum(m, jnp.max(s, axis=0, keepdims=True))
            p = jnp.exp2(s - m_new)
            vt_ext = jnp.concatenate([vt_ref[hs, pl.ds(row0, hk)], ones_rows], axis=0)
            acc_ref[...] = jnp.exp2(m - m_new) * acc_ref[...] + _dot(vt_ext, p.astype(_CD))
            return m_new

        qk(0, sa_stage_ref)

        def body(c, carry, qk=qk, consume=consume):
            off = pl.multiple_of(c * tq, tq)
            qk(off + hk, sb_stage_ref)
            carry = consume(off, sa_stage_ref, carry)
            qk(off + tq, sa_stage_ref)
            return consume(off + hk, sb_stage_ref, carry)

        carry = lax.fori_loop(0, j, body, jnp.full((1, 2 * tq), _NEG, _F32))
        off = pl.multiple_of(j * tq, tq)
        qk(off + hk, sb_stage_ref)
        carry = consume(off, sa_stage_ref, carry, diag_row0=0)
        consume(off + hk, sb_stage_ref, carry, diag_row0=hk)
        acc = acc_ref[...]
        o_t = acc[0:LANES] / acc[LANES:LANES + 1]
        d = (o_t[:, 0:tq] - lam * o_t[:, tq:2 * tq]).T
        o_ref[0, :, hs] = _rms(d, g).astype(o_ref.dtype)


def _diff_attention(lams, g, qt, k, vt, lam_init):
    b, s, w = k.shape
    tq = TQ_DIFF
    nq = s // tq
    return pl.pallas_call(
        functools.partial(_diff_kernel, lam_init=lam_init, tq=tq),
        grid=(b, nq),
        in_specs=[pl.BlockSpec(lams.shape, lambda bi, j: (0, 0)),
                  pl.BlockSpec(g.shape, lambda bi, j: (0, 0)),
                  pl.BlockSpec((w, tq), lambda bi, j: (0, bi * nq + j)),
                  pl.BlockSpec((1, s, w), lambda bi, j: (bi, 0, 0)),
                  pl.BlockSpec((w, s), lambda bi, j: (0, bi))],
        out_specs=pl.BlockSpec((1, tq, w), lambda bi, j: (bi, j, 0)),
        out_shape=jax.ShapeDtypeStruct((b, s, w), _CD),
        scratch_shapes=[pltpu.VMEM((LANES + ONES_ROWS, 2 * tq), _F32),
                        pltpu.VMEM((tq // 2, 2 * tq), _F32),
                        pltpu.VMEM((tq // 2, 2 * tq), _F32)],
        compiler_params=_params("parallel", "arbitrary"),
        name="diff_attention",
    )(lams, g, qt, k, vt)


def _mem_kv_kernel(mem_ref, g_ref, w_ref, k_ref, v_ref):
    d = mem_ref.shape[2]
    mn = _rms(mem_ref[0], g_ref[...]).astype(_CD)
    k_ref[0] = _dot(mn, w_ref[:, 0:d]).astype(k_ref.dtype)
    v_ref[0] = _dot(mn, w_ref[:, d:2 * d]).astype(v_ref.dtype)


def _mem_kv(mem, g, w_kv):
    b, m, d = mem.shape
    blk = pl.BlockSpec((1, m, d), lambda bi: (bi, 0, 0))
    return pl.pallas_call(
        _mem_kv_kernel,
        grid=(b,),
        in_specs=[blk, pl.BlockSpec((1, d), lambda bi: (0, 0)),
                  pl.BlockSpec((d, 2 * d), lambda bi: (0, 0))],
        out_specs=[blk, blk],
        out_shape=[jax.ShapeDtypeStruct((b, m, d), _CD)] * 2,
        compiler_params=_params("parallel"),
        name="mem_kv",
    )(mem, g, w_kv)


def _mix_mem_kernel(h_ref, oa_ref, ob_ref, woa_ref, wob_ref, g_ref, wq_ref, mk_ref, mv_ref,
                    wo_ref, o_ref):
    h1 = h_ref[0] + _dot(oa_ref[0], woa_ref[...]) + _dot(ob_ref[0], wob_ref[...])
    xn = _rms(h1, g_ref[...]).astype(_CD)
    q = _dot(xn, wq_ref[...]).astype(_CD)
    hd = q.shape[1] // MEM_HEADS
    heads = []
    for hh in range(MEM_HEADS):
        hs = slice(hh * hd, (hh + 1) * hd)
        s = _dot_nt(q[:, hs], mk_ref[0, :, hs])
        p = jnp.exp(s - jnp.max(s, axis=1, keepdims=True))
        o = _dot(p.astype(_CD), mv_ref[0, :, hs]) / jnp.sum(p, axis=1, keepdims=True)
        heads.append(o.astype(_CD))
    o_ref[0] = h1 + _dot(jnp.concatenate(heads, axis=1), wo_ref[...])


def _mix_mem(h, oa, ob, woa, wob, g, wq, mk, mv, wo):
    b, s, d = h.shape
    tb = TB_MEM
    tok = lambda w: pl.BlockSpec((1, tb, w), lambda bi, j: (bi, j, 0))
    const = lambda a: pl.BlockSpec(a.shape, lambda bi, j: (0,) * a.ndim)
    memspec = pl.BlockSpec((1,) + mk.shape[1:], lambda bi, j: (bi, 0, 0))
    return pl.pallas_call(
        _mix_mem_kernel,
        grid=(b, s // tb),
        in_specs=[tok(d), tok(oa.shape[2]), tok(ob.shape[2]), const(woa), const(wob), const(g),
                  const(wq), memspec, memspec, const(wo)],
        out_specs=tok(d),
        out_shape=jax.ShapeDtypeStruct((b, s, d), _F32),
        compiler_params=_params("parallel", "parallel"),
        name="mix_mem",
    )(h, oa, ob, woa, wob, g, wq, mk, mv, wo)


def _ffn_kernel(h_ref, halo_ref, g_ref, wg_ref, wu_ref, cw_ref, cb_ref, wd_ref, fg_ref, o_ref, *,
                final_norm):
    j = pl.program_id(1)
    tb = h_ref.shape[1]
    g = g_ref[...]
    h = h_ref[0]
    xn = _rms(h, g).astype(_CD)
    xh = _rms(halo_ref[0], g).astype(_CD)
    gate = _dot(jnp.concatenate([xh, xn], axis=0), wg_ref[...])
    keep = jnp.where(j > 0, 1.0, 0.0)
    gate = jnp.concatenate([gate[0:SUBLANES] * keep, gate[SUBLANES:]], axis=0)
    cw = cw_ref[...]
    conv = cb_ref[...] + cw[CONV_WIDTH - 1:CONV_WIDTH] * gate[SUBLANES:SUBLANES + tb]
    for k in range(CONV_WIDTH - 1):
        sh = CONV_WIDTH - 1 - k
        conv = conv + cw[k:k + 1] * gate[SUBLANES - sh:SUBLANES - sh + tb]
    up = _dot(xn, wu_ref[...])
    act = (conv * jax.nn.sigmoid(conv) * up).astype(_CD)
    out = h + _dot(act, wd_ref[...])
    if final_norm:
        out = _rms(out, fg_ref[...])
    o_ref[0] = out


def _ffn(h, g, wg, wu, cw, cb, wd, fg, final_norm):
    b, s, d = h.shape
    tb = TB_FFN
    const = lambda a: pl.BlockSpec(a.shape, lambda bi, j: (0,) * a.ndim)
    per = tb // SUBLANES
    return pl.pallas_call(
        functools.partial(_ffn_kernel, final_norm=final_norm),
        grid=(b, s // tb),
        in_specs=[pl.BlockSpec((1, tb, d), lambda bi, j: (bi, j, 0)),
                  pl.BlockSpec((1, SUBLANES, d), lambda bi, j: (bi, jnp.maximum(j * per - 1, 0), 0)),
                  const(g), const(wg), const(wu), const(cw), const(cb), const(wd), const(fg)],
        out_specs=pl.BlockSpec((1, tb, d), lambda bi, j: (bi, j, 0)),
        out_shape=jax.ShapeDtypeStruct((b, s, d), _F32),
        compiler_params=_params("parallel", "parallel"),
        name="conv_glu",
    )(h, h, g, wg, wu, cw, cb, wd, fg)


def _prep_w_in(w):
    d = w.shape[0]
    o = 0
    parts = {}
    for name, width in (("qa", A_HEADS * HEAD_DIM), ("ka", HEAD_DIM), ("va", HEAD_DIM),
                        ("qi", IDX_HEADS * HEAD_DIM), ("ki", HEAD_DIM), ("wi", IDX_HEADS),
                        ("qb", 2 * B_HEADS * HEAD_DIM), ("kb", 2 * B_HEADS * HEAD_DIM),
                        ("vb", B_HEADS * B_V_DIM)):
        parts[name] = w[:, o:o + width]
        o += width
    scale = HEAD_DIM ** -0.5 * math.log2(math.e)
    z = lambda n: jnp.zeros((d, n), w.dtype)
    cols = [parts["qa"] * scale, parts["qi"], parts["qb"] * scale, parts["kb"],
            parts["ka"], parts["ki"], parts["vb"],
            parts["va"], z(LANES - HEAD_DIM),
            parts["wi"] * (HEAD_DIM ** -0.5 * IDX_HEADS ** -0.5), z(LANES - IDX_HEADS)]
    return jnp.concatenate(cols, axis=1).astype(_CD)


def kernel(x, mem, positions, mix_norm_g, w_in, lambda_q1, lambda_k1, lambda_q2, lambda_k2,
           diff_norm_g, w_out, mem_attn_norm_g, mem_norm_g, w_q_mem, w_kv_mem, w_o_mem, ffn_norm_g,
           w_gate_up, conv_w, conv_b, w_down, final_norm_g):
    b, s, d = x.shape
    depth = w_in.shape[0]
    d_ff = w_down.shape[1]
    n_sel = min(TOPK_MAX, s // 4)
    wa = A_HEADS * HEAD_DIM
    mem_scale = (d // MEM_HEADS) ** -0.5
    row = lambda v: v.reshape(1, -1).astype(_F32)

    rope = _rope_tables(positions)
    h = x
    for l in range(depth):
        qat, qit, qbt, kb, kk, vbt, vat, wit = _in_proj(
            h.reshape(b * s, d), row(mix_norm_g[l]), _prep_w_in(w_in[l]), rope)
        r3 = lambda t: t.reshape(b, s, t.shape[-1])
        out_a = _dsa_attention(qat, qit, wit, r3(kk), vat, n_sel)
        lam_init = 0.8 - 0.6 * math.exp(-0.3 * l)
        lams = jnp.stack([lambda_q1[l], lambda_k1[l], lambda_q2[l], lambda_k2[l]]).astype(_F32)
        out_b = _diff_attention(lams, row(diff_norm_g[l]), qbt, r3(kb), vbt, lam_init)
        mk, mv = _mem_kv(mem, row(mem_norm_g[l]), w_kv_mem[l].astype(_CD))
        h = _mix_mem(h, out_a, out_b, w_out[l, :wa].astype(_CD), w_out[l, wa:].astype(_CD),
                     row(mem_attn_norm_g[l]), (w_q_mem[l] * mem_scale).astype(_CD), mk, mv,
                     w_o_mem[l].astype(_CD))
        h = _ffn(h, row(ffn_norm_g[l]), w_gate_up[l, :, :d_ff].astype(_CD),
                 w_gate_up[l, :, d_ff:].astype(_CD), conv_w[l].astype(_F32), row(conv_b[l]),
                 w_down[l].astype(_CD), row(final_norm_g), l == depth - 1)
    return h
```
